```python
import math
import jax, jax.numpy as jnp
from jax import lax
import numpy as np

D_MODEL = 1024
BATCH = 8
SEQ = 2048
DEPTH = 1

RET_HEADS = 4
RET_QK_DIM = D_MODEL // RET_HEADS
RET_V_DIM = 2 * RET_QK_DIM
RET_CHUNK = 128
DIFF_HEAD_DIM = 64
DIFF_HEADS = D_MODEL // (2 * DIFF_HEAD_DIM)
DIFF_V_DIM = 2 * DIFF_HEAD_DIM
Q_BLOCK = 128
D_FF = 2816
N_BRANCH = 2
N_SUB = 3
N_MOD = 3
EPS = 1e-6

RET_QK_W = RET_HEADS * RET_QK_DIM
RET_V_W = RET_HEADS * RET_V_DIM
DIFF_QK_W = DIFF_HEADS * 2 * DIFF_HEAD_DIM
DIFF_V_W = DIFF_HEADS * DIFF_V_DIM
GATE_W = N_BRANCH * D_MODEL
IN_W = 2 * RET_QK_W + 2 * RET_V_W + 2 * DIFF_QK_W + DIFF_V_W + GATE_W
SPLIT_POINTS = (
    RET_QK_W,
    2 * RET_QK_W,
    2 * RET_QK_W + RET_V_W,
    2 * RET_QK_W + 2 * RET_V_W,
    2 * RET_QK_W + 2 * RET_V_W + DIFF_QK_W,
    2 * RET_QK_W + 2 * RET_V_W + 2 * DIFF_QK_W,
    2 * RET_QK_W + 2 * RET_V_W + 2 * DIFF_QK_W + DIFF_V_W,
)

kernel_name = "hybrid_retention_diffattn_macaron_adaln"


def lambda_init(layer_idx):
    return 0.8 - 0.6 * math.exp(-0.3 * layer_idx)


def rmsnorm(x, g):
    xf = x.astype(jnp.float32)
    y = xf * lax.rsqrt(jnp.mean(xf * xf, axis=-1, keepdims=True) + EPS)
    return (y * g.astype(jnp.float32)).astype(x.dtype)


def modulate(x, g, shift, scale):
    return rmsnorm(x, g) * (1 + scale[:, None, :]) + shift[:, None, :]


def swiglu(h, w_in, w_out):
    a, b = jnp.split(h @ w_in, 2, axis=-1)
    return (jax.nn.silu(a) * b) @ w_out


def retention_chunkwise(q, k, v):
    B, S, H, dk = q.shape
    dv = v.shape[-1]
    C = RET_CHUNK
    N = S // C
    log_g = jnp.log1p(-jnp.exp2(-5.0 - jnp.arange(H, dtype=jnp.float32)))
    pos = jnp.arange(C, dtype=jnp.float32)
    rel = pos[:, None] - pos[None, :]
    decay_intra = jnp.where(rel[None] >= 0,
                            jnp.exp(jnp.maximum(rel, 0.0)[None] * log_g[:, None, None]),
                            0.0)
    xi = jnp.exp((pos + 1.0)[None, :] * log_g[:, None])[..., None]
    zeta = jnp.exp((C - 1.0 - pos)[None, :] * log_g[:, None])[..., None]
    g_chunk = jnp.exp(C * log_g)[:, None, None]

    def to_chunks(t):
        return t.reshape(B, N, C, H, t.shape[-1]).transpose(1, 0, 3, 2, 4)

    def step(R, inp):
        qc, kc, vc = inp
        intra = jnp.einsum('bhcd,bhsd->bhcs', qc, kc) * decay_intra
        o = (jnp.einsum('bhcs,bhse->bhce', intra, vc)
             + jnp.einsum('bhcd,bhde->bhce', qc, R) * xi)
        R = R * g_chunk + jnp.einsum('bhsd,bhse->bhde', kc * zeta, vc)
        return R, o

    R0 = jnp.zeros((B, H, dk, dv), jnp.float32)
    _, o = lax.scan(step, R0, (to_chunks(q), to_chunks(k), to_chunks(v)))
    return o.transpose(1, 0, 3, 2, 4).reshape(B, S, H, dv)


def diff_attention(q, k, v, lam):
    B, H, _, S, dh = q.shape
    dv = v.shape[-1]
    nb = S // Q_BLOCK
    scale = dh ** -0.5
    slopes = jnp.exp2(-8.0 * (jnp.arange(H, dtype=jnp.float32) + 1.0) / H)
    kpos = jnp.arange(S)

    def block(i):
        start = i * Q_BLOCK
        qb = lax.dynamic_slice_in_dim(q, start, Q_BLOCK, axis=3)
        qpos = start + jnp.arange(Q_BLOCK)
        dist = (qpos[:, None] - kpos[None, :]).astype(jnp.float32)
        bias = -slopes[:, None, None] * dist
        s = jnp.einsum('bhmqd,bhmkd->bhmqk', qb, k) * scale + bias[None, :, None]
        s = jnp.where(dist >= 0, s, -jnp.inf)
        p = jax.nn.softmax(s, axis=-1)
        a = p[:, :, 0] - lam * p[:, :, 1]
        return jnp.einsum('bhqk,bhkd->bhqd', a, v)

    o = lax.map(block, jnp.arange(nb))
    return o.transpose(1, 0, 3, 2, 4).reshape(B, S, H, dv)


def setup_inputs(seed: int = 0) -> dict:
    key = jax.random.key(seed)
    ks = jax.random.split(key, 18)
    L, D, F = DEPTH, D_MODEL, D_FF
    nrm = lambda k, shape, fan_in: jax.random.normal(k, shape, jnp.float32) * fan_in ** -0.5
    return {
        "x": jax.random.normal(ks[0], (BATCH, SEQ, D), jnp.float32),
        "c": jax.random.normal(ks[1], (BATCH, D), jnp.float32),
        "w_cond": nrm(ks[2], (L, D, N_SUB * N_MOD * D), D),
        "b_cond": 0.01 * jax.random.normal(ks[3], (L, N_SUB * N_MOD * D), jnp.float32),
        "g_norm": 1.0 + 0.02 * jax.random.normal(ks[4], (L, N_SUB, D), jnp.float32),
        "w_ffn1_in": nrm(ks[5], (L, D, 2 * F), D),
        "w_ffn1_out": nrm(ks[6], (L, F, D), F),
        "w_in": nrm(ks[7], (L, D, IN_W), D),
        "w_ret_out": nrm(ks[8], (L, RET_V_W, D), RET_V_W),
        "diff_lambda": 0.1 * jax.random.normal(ks[9], (L, 4, DIFF_HEAD_DIM), jnp.float32),
        "diff_subln": 1.0 + 0.02 * jax.random.normal(ks[10], (L, DIFF_V_DIM), jnp.float32),
        "w_diff_out": nrm(ks[11], (L, DIFF_V_W, D), DIFF_V_W),
        "w_out": nrm(ks[12], (L, D, D), D),
        "w_ffn2_in": nrm(ks[13], (L, D, 2 * F), D),
        "w_ffn2_out": nrm(ks[14], (L, F, D), F),
        "g_final": 1.0 + 0.02 * jax.random.normal(ks[15], (D,), jnp.float32),
    }


def reference(x, c, w_cond, b_cond, g_norm, w_ffn1_in, w_ffn1_out, w_in, w_ret_out,
              diff_lambda, diff_subln, w_diff_out, w_out, w_ffn2_in, w_ffn2_out, g_final):
    B, S, D = x.shape
    c_act = jax.nn.silu(c)
    for l in range(DEPTH):
        mod = (c_act @ w_cond[l] + b_cond[l]).reshape(B, N_SUB, N_MOD, D)

        h = modulate(x, g_norm[l, 0], mod[:, 0, 0], mod[:, 0, 1])
        x = x + 0.5 * mod[:, 0, 2][:, None, :] * swiglu(h, w_ffn1_in[l], w_ffn1_out[l])

        h = modulate(x, g_norm[l, 1], mod[:, 1, 0], mod[:, 1, 1])
        rq, rk, rv, rg, dq, dk_, dv_, gates = jnp.split(h @ w_in[l], SPLIT_POINTS, axis=-1)

        f32 = jnp.float32
        rq = rq.reshape(B, S, RET_HEADS, RET_QK_DIM).astype(f32)
        rk = rk.reshape(B, S, RET_HEADS, RET_QK_DIM).astype(f32) * RET_QK_DIM ** -0.5
        rv = rv.reshape(B, S, RET_HEADS, RET_V_DIM).astype(f32)
        ro = retention_chunkwise(rq, rk, rv)
        mu = jnp.mean(ro, axis=-1, keepdims=True)
        var = jnp.mean(jnp.square(ro - mu), axis=-1, keepdims=True)
        ro = ((ro - mu) * lax.rsqrt(var + EPS)).reshape(B, S, RET_V_W).astype(x.dtype)
        y_ret = (jax.nn.silu(rg) * ro) @ w_ret_out[l]

        lam_init = lambda_init(l)
        lp = diff_lambda[l].astype(f32)
        lam = jnp.exp(jnp.sum(lp[0] * lp[1])) - jnp.exp(jnp.sum(lp[2] * lp[3])) + lam_init
        dq = dq.reshape(B, S, DIFF_HEADS, 2, DIFF_HEAD_DIM).transpose(0, 2, 3, 1, 4).astype(f32)
        dk_ = dk_.reshape(B, S, DIFF_HEADS, 2, DIFF_HEAD_DIM).transpose(0, 2, 3, 1, 4).astype(f32)
        dv_ = dv_.reshape(B, S, DIFF_HEADS, DIFF_V_DIM).transpose(0, 2, 1, 3).astype(f32)
        do = diff_attention(dq, dk_, dv_, lam)
        do = rmsnorm(do, diff_subln[l]) * (1.0 - lam_init)
        y_diff = do.reshape(B, S, DIFF_V_W).astype(x.dtype) @ w_diff_out[l]

        gates = jax.nn.sigmoid(gates.reshape(B, S, N_BRANCH, D))
        y = gates[:, :, 0] * y_ret + gates[:, :, 1] * y_diff
        x = x + mod[:, 1, 2][:, None, :] * (y @ w_out[l])

        h = modulate(x, g_norm[l, 2], mod[:, 2, 0], mod[:, 2, 1])
        x = x + 0.5 * mod[:, 2, 2][:, None, :] * swiglu(h, w_ffn2_in[l], w_ffn2_out[l])

    return rmsnorm(x, g_final)
```

```python
import functools
import math

import jax
import jax.numpy as jnp
from jax import lax
from jax.experimental import pallas as pl
from jax.experimental.pallas import tpu as pltpu

F32 = jnp.float32
BF16 = jnp.bfloat16
EPS = 1e-6
NEG = -1e30

RET_HEADS = 4
RET_CHUNK = 128
DIFF_HEAD_DIM = 64
N_SUB = 3
N_MOD = 3

VMEM_LIMIT_BYTES = 56 * 1024 * 1024


def _resident(shape):
    nd = len(shape)
    return pl.BlockSpec(shape, lambda *_: (0,) * nd, pipeline_mode=pl.Buffered(1))


def _params(*sem):
    return pltpu.CompilerParams(dimension_semantics=sem, vmem_limit_bytes=VMEM_LIMIT_BYTES)


def _silu(a):
    return a / (1.0 + jnp.exp(-a))


def _sigmoid(a):
    return 1.0 / (1.0 + jnp.exp(-a))


def _modulated(x, mod_ref, g_ref, sub):
    shift = mod_ref[0, N_MOD * sub:N_MOD * sub + 1, :]
    scale = mod_ref[0, N_MOD * sub + 1:N_MOD * sub + 2, :]
    ms = jnp.mean(x * x, axis=-1, keepdims=True)
    h = x * lax.rsqrt(ms + EPS) * g_ref[sub:sub + 1, :]
    return (h * (1.0 + scale) + shift).astype(BF16)


def _mod_kernel(c_ref, w_ref, b_ref, o_ref):
    c_act = _silu(c_ref[...]).astype(BF16)
    o_ref[...] = jnp.dot(c_act, w_ref[...].astype(BF16), preferred_element_type=F32) + b_ref[...]


def _mod(c, w, b):
    B, D = c.shape
    N = w.shape[1]
    tn = 1024
    return pl.pallas_call(
        _mod_kernel,
        out_shape=jax.ShapeDtypeStruct((B, N), F32),
        grid=(N // tn,),
        in_specs=[pl.BlockSpec((B, D), lambda j: (0, 0)),
                  pl.BlockSpec((D, tn), lambda j: (0, j)),
                  pl.BlockSpec((1, tn), lambda j: (0, j))],
        out_specs=pl.BlockSpec((B, tn), lambda j: (0, j)),
        compiler_params=_params("arbitrary"),
        name="mod",
    )(c, w, b.reshape(1, N))


def _ffn_kernel(x_ref, mod_ref, g_ref, win_ref, wout_ref, *rest, sub, d_ff, tf, final):
    if final:
        gfin_ref, o_ref, act_ref = rest
    else:
        o_ref, act_ref = rest
    x = x_ref[...]
    h = _modulated(x, mod_ref, g_ref, sub)
    for c in range(d_ff // tf):
        a = jnp.dot(h, win_ref[:, c * tf:(c + 1) * tf], preferred_element_type=F32)
        b = jnp.dot(h, win_ref[:, d_ff + c * tf:d_ff + (c + 1) * tf], preferred_element_type=F32)
        act_ref[:, c * tf:(c + 1) * tf] = (_silu(a) * b).astype(BF16)
    y = jnp.dot(act_ref[...], wout_ref[...], preferred_element_type=F32)
    gate = mod_ref[0, N_MOD * sub + 2:N_MOD * sub + 3, :]
    out = x + (0.5 * gate) * y
    if final:
        ms = jnp.mean(out * out, axis=-1, keepdims=True)
        out = out * lax.rsqrt(ms + EPS) * gfin_ref[...]
    o_ref[...] = out


def _ffn(x2d, mod3, g_norm, w_in, w_out, g_final, *, sub, seq, tm=512, tf=256):
    M, D = x2d.shape
    d_ff = w_out.shape[0]
    tiles_per_seq = seq // tm
    final = g_final is not None
    in_specs = [pl.BlockSpec((tm, D), lambda i: (i, 0)),
                pl.BlockSpec((1, N_SUB * N_MOD, D), lambda i: (i // tiles_per_seq, 0, 0)),
                _resident(g_norm.shape),
                _resident(w_in.shape),
                _resident(w_out.shape)]
    args = [x2d, mod3, g_norm, w_in, w_out]
    if final:
        in_specs.append(_resident((1, D)))
        args.append(g_final.reshape(1, D))
    return pl.pallas_call(
        functools.partial(_ffn_kernel, sub=sub, d_ff=d_ff, tf=tf, final=final),
        out_shape=jax.ShapeDtypeStruct((M, D), F32),
        grid=(M // tm,),
        in_specs=in_specs,
        out_specs=pl.BlockSpec((tm, D), lambda i: (i, 0)),
        scratch_shapes=[pltpu.VMEM((tm, d_ff), BF16)],
        compiler_params=_params("arbitrary"),
        name="ffn%d" % sub,
    )(*args)


def _inproj_segments(D):
    rqk, rv = D, 2 * D
    return (("rq", rqk, None), ("rk", rqk, "ret_k_scale"), ("rv", rv, None), ("rg", rv, "silu"),
            ("dq", D, "diff_q_scale"), ("dk", D, None), ("dv", D, None), ("gates", 2 * D, "sigmoid"))


def _inproj_kernel(x_ref, mod_ref, g_ref, w_ref, *out_refs, segs, tn, ret_k_scale, diff_q_scale):
    h = _modulated(x_ref[...], mod_ref, g_ref, 1)
    col = 0
    for (_, width, epi), o_ref in zip(segs, out_refs):
        for c in range(width // tn):
            y = jnp.dot(h, w_ref[:, col + c * tn:col + (c + 1) * tn], preferred_element_type=F32)
            if epi == "silu":
                y = _silu(y)
            elif epi == "sigmoid":
                y = _sigmoid(y)
            elif epi == "ret_k_scale":
                y = y * ret_k_scale
            elif epi == "diff_q_scale":
                y = y * diff_q_scale
            o_ref[:, c * tn:(c + 1) * tn] = y.astype(BF16)
        col += width


def _inproj(x2d, mod3, g_norm, w_in, *, seq, tm=256, tn=512):
    M, D = x2d.shape
    segs = _inproj_segments(D)
    assert sum(s[1] for s in segs) == w_in.shape[1]
    tiles_per_seq = seq // tm
    kern = functools.partial(
        _inproj_kernel, segs=segs, tn=tn,
        ret_k_scale=float((D // RET_HEADS) ** -0.5), diff_q_scale=float(DIFF_HEAD_DIM ** -0.5))
    return pl.pallas_call(
        kern,
        out_shape=[jax.ShapeDtypeStruct((M, s[1]), BF16) for s in segs],
        grid=(M // tm,),
        in_specs=[pl.BlockSpec((tm, D), lambda i: (i, 0)),
                  pl.BlockSpec((1, N_SUB * N_MOD, D), lambda i: (i // tiles_per_seq, 0, 0)),
                  _resident(g_norm.shape),
                  _resident(w_in.shape)],
        out_specs=[pl.BlockSpec((tm, s[1]), lambda i: (i, 0)) for s in segs],
        compiler_params=_params("arbitrary"),
        name="inproj",
    )(x2d, mod3, g_norm, w_in)


def _ret_kernel(q_ref, k_ref, v_ref, sg_ref, dec_ref, xi_ref, zeta_ref, gch_ref, o_ref, state_ref,
                *, heads, dk, dv):
    @pl.when(pl.program_id(1) == 0)
    def _():
        state_ref[...] = jnp.zeros_like(state_ref)

    for h in range(heads):
        q = q_ref[0, :, h * dk:(h + 1) * dk]
        k = k_ref[0, :, h * dk:(h + 1) * dk]
        v = v_ref[0, :, h * dv:(h + 1) * dv]
        s = lax.dot_general(q, k, (((1,), (1,)), ((), ())), preferred_element_type=F32)
        intra = (s * dec_ref[h]).astype(BF16)
        state = state_ref[h]
        o = (jnp.dot(intra, v, preferred_element_type=F32)
             + jnp.dot(q, state.astype(BF16), preferred_element_type=F32) * xi_ref[h])
        kz_t = (k.astype(F32) * zeta_ref[h]).T.astype(BF16)
        state_ref[h] = state * gch_ref[h] + jnp.dot(kz_t, v, preferred_element_type=F32)
        mu = jnp.mean(o, axis=-1, keepdims=True)
        d = o - mu
        var = jnp.mean(d * d, axis=-1, keepdims=True)
        on = d * lax.rsqrt(var + EPS)
        o_ref[0, :, h * dv:(h + 1) * dv] = (on * sg_ref[0, :, h * dv:(h + 1) * dv].astype(F32)).astype(BF16)


def _retention_tables(heads, C):
    hh = jnp.arange(heads, dtype=F32)
    log_g = jnp.log1p(-jnp.exp2(-5.0 - hh))
    pos = jnp.arange(C, dtype=F32)
    rel = pos[:, None] - pos[None, :]
    decay = jnp.where(rel[None] >= 0, jnp.exp(jnp.maximum(rel, 0.0)[None] * log_g[:, None, None]), 0.0)
    xi = jnp.exp((pos + 1.0)[None, :] * log_g[:, None])[..., None]
    zeta = jnp.exp((C - 1.0 - pos)[None, :] * log_g[:, None])[..., None]
    g_chunk = jnp.exp(C * log_g)[:, None, None]
    return decay, xi, zeta, g_chunk


def _retention(rq, rk, rv, sg, *, heads, C):
    B, S, qk_w = rq.shape
    v_w = rv.shape[-1]
    dk, dv = qk_w // heads, v_w // heads
    decay, xi, zeta, g_chunk = _retention_tables(heads, C)
    qk_spec = pl.BlockSpec((1, C, qk_w), lambda b, n: (b, n, 0))
    v_spec = pl.BlockSpec((1, C, v_w), lambda b, n: (b, n, 0))
    return pl.pallas_call(
        functools.partial(_ret_kernel, heads=heads, dk=dk, dv=dv),
        out_shape=jax.ShapeDtypeStruct((B, S, v_w), BF16),
        grid=(B, S // C),
        in_specs=[qk_spec, qk_spec, v_spec, v_spec,
                  _resident(decay.shape), _resident(xi.shape), _resident(zeta.shape),
                  _resident(g_chunk.shape)],
        out_specs=v_spec,
        scratch_shapes=[pltpu.VMEM((heads, dk, dv), F32)],
        compiler_params=_params("arbitrary", "arbitrary"),
        name="retention",
    )(rq, rk, rv, sg, decay, xi, zeta, g_chunk)


def _diff_kernel(q_ref, k_ref, v_ref, lam_ref, subln_ref, slope_ref, o_ref, vt_ref,
                 *, T, seq, dh, lam_init):
    nblk = seq // T
    dv = 2 * dh
    slope = slope_ref[0]
    lp = lam_ref[...]
    lam = (jnp.exp(jnp.sum(lp[0:1] * lp[1:2], axis=-1, keepdims=True))
           - jnp.exp(jnp.sum(lp[2:3] * lp[3:4], axis=-1, keepdims=True)) + lam_init)

    for j in range(nblk):
        vt_ref[j] = v_ref[0, j * T:(j + 1) * T, :].astype(F32).T.astype(BF16)

    key_id = lax.broadcasted_iota(jnp.int32, (T, 2 * T), 0)
    col_id = lax.broadcasted_iota(jnp.int32, (T, 2 * T), 1)
    qry_id = jnp.where(col_id >= T, col_id - T, col_id)
    causal = key_id <= qry_id
    bias0 = slope * key_id.astype(F32)
    feat_id = lax.broadcasted_iota(jnp.int32, (dv, T), 0)

    def kv_step(j, qi, wq, carry, masked):
        m, l, acc = carry
        kj = k_ref[0, pl.ds(pl.multiple_of(j * T, T), T), :]
        t = jnp.dot(kj, wq, preferred_element_type=F32) + bias0
        if masked:
            t = jnp.where(causal, t, NEG)
        shift = slope * ((j - qi) * T).astype(F32)
        m_new = jnp.maximum(m, jnp.max(t, axis=0, keepdims=True) + shift)
        alpha = jnp.exp(m - m_new)
        p = jnp.exp(t + (shift - m_new))
        l = alpha * l + jnp.sum(p, axis=0, keepdims=True)
        acc = alpha * acc + jnp.dot(vt_ref[j], p.astype(BF16), preferred_element_type=F32)
        return m_new, l, acc

    def q_tile(qi, _):
        q0 = pl.multiple_of(qi * T, T)
        qt = q_ref[0, pl.ds(q0, T), :].astype(F32).T
        wq = jnp.concatenate([jnp.where(feat_id < dh, qt, 0.0), jnp.where(feat_id >= dh, qt, 0.0)],
                             axis=1).astype(BF16)
        carry = (jnp.full((1, 2 * T), NEG, F32), jnp.zeros((1, 2 * T), F32), jnp.zeros((dv, 2 * T), F32))
        carry = lax.fori_loop(0, qi, lambda j, c: kv_step(j, qi, wq, c, False), carry)
        m, l, acc = kv_step(qi, qi, wq, carry, True)
        r = 1.0 / l
        out = acc[:, :T] * r[:, :T] - lam * (acc[:, T:] * r[:, T:])
        ms = jnp.mean(out * out, axis=0, keepdims=True)
        out = out * lax.rsqrt(ms + EPS) * subln_ref[...] * (1.0 - lam_init)
        o_ref[0, pl.ds(q0, T), :] = out.T.astype(BF16)
        return 0

    lax.fori_loop(0, nblk, q_tile, 0)


def _diff_attention(dq, dk, dv, diff_lambda, subln, *, heads, lam_init, T=256):
    B, S, W = dq.shape
    hw = W // heads
    dh = hw // 2
    slopes = jnp.exp2(-8.0 * (jnp.arange(heads, dtype=F32) + 1.0) / heads).reshape(heads, 1, 1)
    head_spec = pl.BlockSpec((1, S, hw), lambda b, h: (b, 0, h))
    return pl.pallas_call(
        functools.partial(_diff_kernel, T=T, seq=S, dh=dh, lam_init=lam_init),
        out_shape=jax.ShapeDtypeStruct((B, S, W), BF16),
        grid=(B, heads),
        in_specs=[head_spec, head_spec, head_spec,
                  _resident(diff_lambda.shape),
                  _resident((hw, 1)),
                  pl.BlockSpec((1, 1, 1), lambda b, h: (h, 0, 0))],
        out_specs=head_spec,
        scratch_shapes=[pltpu.VMEM((S // T, hw, T), BF16)],
        compiler_params=_params("arbitrary", "arbitrary"),
        name="diffattn",
    )(dq, dk, dv, diff_lambda, subln.reshape(hw, 1), slopes)


def _mixout_kernel(x_ref, mod_ref, ret_ref, dif_ref, gs_ref, wro_ref, wdo_ref, wo_ref, o_ref):
    D = x_ref.shape[-1]
    y_ret = jnp.dot(ret_ref[...], wro_ref[...], preferred_element_type=F32)
    y_dif = jnp.dot(dif_ref[...], wdo_ref[...], preferred_element_type=F32)
    y = gs_ref[:, :D].astype(F32) * y_ret + gs_ref[:, D:].astype(F32) * y_dif
    gate = mod_ref[0, N_MOD * 1 + 2:N_MOD * 1 + 3, :]
    o_ref[...] = x_ref[...] + gate * jnp.dot(y.astype(BF16), wo_ref[...], preferred_element_type=F32)


def _mixout(x2d, mod3, ret2d, dif2d, gs2d, w_ro, w_do, w_o, *, seq, tm=512):
    M, D = x2d.shape
    tiles_per_seq = seq // tm
    row = lambda w: pl.BlockSpec((tm, w), lambda i: (i, 0))
    return pl.pallas_call(
        _mixout_kernel,
        out_shape=jax.ShapeDtypeStruct((M, D), F32),
        grid=(M // tm,),
        in_specs=[row(D),
                  pl.BlockSpec((1, N_SUB * N_MOD, D), lambda i: (i // tiles_per_seq, 0, 0)),
                  row(ret2d.shape[1]), row(dif2d.shape[1]), row(gs2d.shape[1]),
                  _resident(w_ro.shape), _resident(w_do.shape), _resident(w_o.shape)],
        out_specs=row(D),
        compiler_params=_params("arbitrary"),
        name="mixout",
    )(x2d, mod3, ret2d, dif2d, gs2d, w_ro, w_do, w_o)


def kernel(x, c, w_cond, b_cond, g_norm, w_ffn1_in, w_ffn1_out, w_in, w_ret_out, diff_lambda,
           diff_subln, w_diff_out, w_out, w_ffn2_in, w_ffn2_out, g_final):
    B, S, D = x.shape
    depth = w_cond.shape[0]
    diff_heads = D // (2 * DIFF_HEAD_DIM)
    xf = x.reshape(B * S, D)
    for l in range(depth):
        lam_init = 0.8 - 0.6 * math.exp(-0.3 * l)
        mod3 = _mod(c, w_cond[l], b_cond[l]).reshape(B, N_SUB * N_MOD, D)
        last = l == depth - 1

        xf = _ffn(xf, mod3, g_norm[l], w_ffn1_in[l].astype(BF16), w_ffn1_out[l].astype(BF16), None,
                  sub=0, seq=S)

        rq, rk, rv, sg, dq, dk, dv, gs = _inproj(xf, mod3, g_norm[l], w_in[l].astype(BF16), seq=S)
        b3 = lambda t: t.reshape(B, S, t.shape[-1])
        ret = _retention(b3(rq), b3(rk), b3(rv), b3(sg), heads=RET_HEADS, C=RET_CHUNK)
        dif = _diff_attention(b3(dq), b3(dk), b3(dv), diff_lambda[l], diff_subln[l],
                              heads=diff_heads, lam_init=lam_init)
        xf = _mixout(xf, mod3, ret.reshape(B * S, -1), dif.reshape(B * S, -1), gs,
                     w_ret_out[l].astype(BF16), w_diff_out[l].astype(BF16), w_out[l].astype(BF16), seq=S)

        xf = _ffn(xf, mod3, g_norm[l], w_ffn2_in[l].astype(BF16), w_ffn2_out[l].astype(BF16),
                  g_final if last else None, sub=2, seq=S)
    if depth == 0:
        raise ValueError("depth must be positive")
    return xf.reshape(B, S, D)
```

```python
import functools
import math

import jax
import jax.numpy as jnp
from jax import lax
from jax.experimental import pallas as pl
from jax.experimental.pallas import tpu as pltpu

F32 = jnp.float32
BF16 = jnp.bfloat16
EPS = 1e-6
NEG = -1e30
LOG2E = math.log2(math.e)

RET_HEADS = 4
RET_CHUNK = 128
DIFF_HEAD_DIM = 64
N_SUB = 3
N_MOD = 3

VMEM_LIMIT_BYTES = 56 * 1024 * 1024


def _resident(shape):
    nd = len(shape)
    return pl.BlockSpec(shape, lambda *_: (0,) * nd, pipeline_mode=pl.Buffered(1))


def _params(*sem):
    return pltpu.CompilerParams(dimension_semantics=sem, vmem_limit_bytes=VMEM_LIMIT_BYTES)


def _silu(a):
    return a / (1.0 + jnp.exp(-a))


def _sigmoid(a):
    return 1.0 / (1.0 + jnp.exp(-a))


def _modulated(x, mod_ref, g_ref, sub):
    shift = mod_ref[0, N_MOD * sub:N_MOD * sub + 1, :]
    scale = mod_ref[0, N_MOD * sub + 1:N_MOD * sub + 2, :]
    ms = jnp.mean(x * x, axis=-1, keepdims=True)
    h = x * lax.rsqrt(ms + EPS) * g_ref[sub:sub + 1, :]
    return (h * (1.0 + scale) + shift).astype(BF16)


def _mod_kernel(c_ref, w_ref, b_ref, o_ref):
    c_act = _silu(c_ref[...]).astype(BF16)
    o_ref[...] = jnp.dot(c_act, w_ref[...].astype(BF16), preferred_element_type=F32) + b_ref[...]


def _mod(c, w, b):
    B, D = c.shape
    N = w.shape[1]
    tn = 1024
    return pl.pallas_call(
        _mod_kernel,
        out_shape=jax.ShapeDtypeStruct((B, N), F32),
        grid=(N // tn,),
        in_specs=[pl.BlockSpec((B, D), lambda j: (0, 0)),
                  pl.BlockSpec((D, tn), lambda j: (0, j)),
                  pl.BlockSpec((1, tn), lambda j: (0, j))],
        out_specs=pl.BlockSpec((B, tn), lambda j: (0, j)),
        compiler_params=_params("arbitrary"),
        name="mod",
    )(c, w, b.reshape(1, N))


def _ffn_kernel(x_ref, mod_ref, g_ref, win_ref, wout_ref, *rest, sub, d_ff, tf, final):
    if final:
        gfin_ref, o_ref, act_ref = rest
    else:
        o_ref, act_ref = rest
    x = x_ref[...]
    h = _modulated(x, mod_ref, g_ref, sub)
    for c in range(d_ff // tf):
        a = jnp.dot(h, win_ref[:, c * tf:(c + 1) * tf], preferred_element_type=F32)
        b = jnp.dot(h, win_ref[:, d_ff + c * tf:d_ff + (c + 1) * tf], preferred_element_type=F32)
        act_ref[:, c * tf:(c + 1) * tf] = (_silu(a) * b).astype(BF16)
    y = jnp.dot(act_ref[...], wout_ref[...], preferred_element_type=F32)
    gate = mod_ref[0, N_MOD * sub + 2:N_MOD * sub + 3, :]
    out = x + (0.5 * gate) * y
    if final:
        ms = jnp.mean(out * out, axis=-1, keepdims=True)
        out = out * lax.rsqrt(ms + EPS) * gfin_ref[...]
    o_ref[...] = out


def _ffn(x2d, mod3, g_norm, w_in, w_out, g_final, *, sub, seq, tm=512, tf=256):
    M, D = x2d.shape
    d_ff = w_out.shape[0]
    tiles_per_seq = seq // tm
    final = g_final is not None
    in_specs = [pl.BlockSpec((tm, D), lambda i: (i, 0)),
                pl.BlockSpec((1, N_SUB * N_MOD, D), lambda i: (i // tiles_per_seq, 0, 0)),
                _resident(g_norm.shape),
                _resident(w_in.shape),
                _resident(w_out.shape)]
    args = [x2d, mod3, g_norm, w_in, w_out]
    if final:
        in_specs.append(_resident((1, D)))
        args.append(g_final.reshape(1, D))
    return pl.pallas_call(
        functools.partial(_ffn_kernel, sub=sub, d_ff=d_ff, tf=tf, final=final),
        out_shape=jax.ShapeDtypeStruct((M, D), F32),
        grid=(M // tm,),
        in_specs=in_specs,
        out_specs=pl.BlockSpec((tm, D), lambda i: (i, 0)),
        scratch_shapes=[pltpu.VMEM((tm, d_ff), BF16)],
        compiler_params=_params("arbitrary"),
        name="ffn%d" % sub,
    )(*args)


def _inproj_segments(D):
    rqk, rv = D, 2 * D
    return (("rq", rqk, None), ("rk", rqk, "ret_k_scale"), ("rv", rv, None), ("rg", rv, "silu"),
            ("dq", D, "diff_q_scale"), ("dk", D, None), ("dv", D, None), ("gates", 2 * D, "sigmoid"))


def _inproj_kernel(x_ref, mod_ref, g_ref, w_ref, *out_refs, segs, tn, ret_k_scale, diff_q_scale):
    h = _modulated(x_ref[...], mod_ref, g_ref, 1)
    col = 0
    for (_, width, epi), o_ref in zip(segs, out_refs):
        for c in range(width // tn):
            y = jnp.dot(h, w_ref[:, col + c * tn:col + (c + 1) * tn], preferred_element_type=F32)
            if epi == "silu":
                y = _silu(y)
            elif epi == "sigmoid":
                y = _sigmoid(y)
            elif epi == "ret_k_scale":
                y = y * ret_k_scale
            elif epi == "diff_q_scale":
                y = y * diff_q_scale
            o_ref[:, c * tn:(c + 1) * tn] = y.astype(BF16)
        col += width


def _inproj(x2d, mod3, g_norm, w_in, *, seq, tm=256, tn=512):
    M, D = x2d.shape
    segs = _inproj_segments(D)
    assert sum(s[1] for s in segs) == w_in.shape[1]
    tiles_per_seq = seq // tm
    kern = functools.partial(
        _inproj_kernel, segs=segs, tn=tn,
        ret_k_scale=float((D // RET_HEADS) ** -0.5), diff_q_scale=LOG2E * DIFF_HEAD_DIM ** -0.5)
    return pl.pallas_call(
        kern,
        out_shape=[jax.ShapeDtypeStruct((M, s[1]), BF16) for s in segs],
        grid=(M // tm,),
        in_specs=[pl.BlockSpec((tm, D), lambda i: (i, 0)),
                  pl.BlockSpec((1, N_SUB * N_MOD, D), lambda i: (i // tiles_per_seq, 0, 0)),
                  _resident(g_norm.shape),
                  _resident(w_in.shape)],
        out_specs=[pl.BlockSpec((tm, s[1]), lambda i: (i, 0)) for s in segs],
        compiler_params=_params("arbitrary"),
        name="inproj",
    )(x2d, mod3, g_norm, w_in)


def _ret_kernel(q_ref, k_ref, v_ref, sg_ref, dec_ref, xi_ref, zeta_ref, gch_ref, o_ref, state_ref,
                *, heads, dk, dv):
    @pl.when(pl.program_id(1) == 0)
    def _():
        state_ref[...] = jnp.zeros_like(state_ref)

    for h in range(heads):
        q = q_ref[0, :, h * dk:(h + 1) * dk]
        k = k_ref[0, :, h * dk:(h + 1) * dk]
        v = v_ref[0, :, h * dv:(h + 1) * dv]
        s = lax.dot_general(q, k, (((1,), (1,)), ((), ())), preferred_element_type=F32)
        intra = (s * dec_ref[h]).astype(BF16)
        state = state_ref[h]
        o = (jnp.dot(intra, v, preferred_element_type=F32)
             + jnp.dot(q, state.astype(BF16), preferred_element_type=F32) * xi_ref[h])
        kz_t = (k.astype(F32) * zeta_ref[h]).T.astype(BF16)
        state_ref[h] = state * gch_ref[h] + jnp.dot(kz_t, v, preferred_element_type=F32)
        mu = jnp.mean(o, axis=-1, keepdims=True)
        d = o - mu
        var = jnp.mean(d * d, axis=-1, keepdims=True)
        on = d * lax.rsqrt(var + EPS)
        o_ref[0, :, h * dv:(h + 1) * dv] = (on * sg_ref[0, :, h * dv:(h + 1) * dv].astype(F32)).astype(BF16)


def _retention_tables(heads, C):
    hh = jnp.arange(heads, dtype=F32)
    log_g = jnp.log1p(-jnp.exp2(-5.0 - hh))
    pos = jnp.arange(C, dtype=F32)
    rel = pos[:, None] - pos[None, :]
    decay = jnp.where(rel[None] >= 0, jnp.exp(jnp.maximum(rel, 0.0)[None] * log_g[:, None, None]), 0.0)
    xi = jnp.exp((pos + 1.0)[None, :] * log_g[:, None])[..., None]
    zeta = jnp.exp((C - 1.0 - pos)[None, :] * log_g[:, None])[..., None]
    g_chunk = jnp.exp(C * log_g)[:, None, None]
    return decay, xi, zeta, g_chunk


def _retention(rq, rk, rv, sg, *, heads, C):
    B, S, qk_w = rq.shape
    v_w = rv.shape[-1]
    dk, dv = qk_w // heads, v_w // heads
    decay, xi, zeta, g_chunk = _retention_tables(heads, C)
    qk_spec = pl.BlockSpec((1, C, qk_w), lambda b, n: (b, n, 0))
    v_spec = pl.BlockSpec((1, C, v_w), lambda b, n: (b, n, 0))
    return pl.pallas_call(
        functools.partial(_ret_kernel, heads=heads, dk=dk, dv=dv),
        out_shape=jax.ShapeDtypeStruct((B, S, v_w), BF16),
        grid=(B, S // C),
        in_specs=[qk_spec, qk_spec, v_spec, v_spec,
                  _resident(decay.shape), _resident(xi.shape), _resident(zeta.shape),
                  _resident(g_chunk.shape)],
        out_specs=v_spec,
        scratch_shapes=[pltpu.VMEM((heads, dk, dv), F32)],
        compiler_params=_params("arbitrary", "arbitrary"),
        name="retention",
    )(rq, rk, rv, sg, decay, xi, zeta, g_chunk)


def _diff_kernel(q_ref, k_ref, v_ref, lam_ref, subln_ref, slope_ref, o_ref,
                 vt_ref, wq_ref, bias_ref, t_ref, p_ref, acc_ref, m_ref, l_ref, alpha_ref,
                 *, T, seq, dh, hpg, lam_init):
    nblk = seq // T
    nsteps = nblk * (nblk + 1) // 2
    dv = 2 * dh
    lp = lam_ref[...]
    lam = (jnp.exp(jnp.sum(lp[0:1] * lp[1:2], axis=-1, keepdims=True))
           - jnp.exp(jnp.sum(lp[2:3] * lp[3:4], axis=-1, keepdims=True)) + lam_init)

    key_id = lax.broadcasted_iota(jnp.int32, (T, 2 * T), 0)
    col_id = lax.broadcasted_iota(jnp.int32, (T, 2 * T), 1)
    causal = key_id <= jnp.where(col_id >= T, col_id - T, col_id)
    key_pos = key_id.astype(F32)
    feat_id = lax.broadcasted_iota(jnp.int32, (dv, T), 0)
    for i in range(hpg):
        slope_pos = slope_ref[i] * key_pos
        bias_ref[i, 0] = slope_pos
        bias_ref[i, 1] = jnp.where(causal, slope_pos, NEG)
        for j in range(nblk):
            rows = slice(j * T, (j + 1) * T)
            cols = slice(i * dv, (i + 1) * dv)
            vt_ref[i, j] = v_ref[0, rows, cols].astype(F32).T.astype(BF16)
            qt = q_ref[0, rows, cols].astype(F32).T
            wq_ref[i, j] = jnp.concatenate([jnp.where(feat_id < dh, qt, 0.0),
                                            jnp.where(feat_id >= dh, qt, 0.0)], axis=1).astype(BF16)
        acc_ref[i] = jnp.zeros((dv, 2 * T), F32)
        alpha_ref[i] = jnp.zeros((1, 2 * T), F32)
    m_ref[...] = jnp.zeros_like(m_ref)
    l_ref[...] = jnp.zeros_like(l_ref)
    p_ref[1] = jnp.zeros((hpg, T, 2 * T), BF16)

    def scores(qi, j):
        qi = jnp.minimum(qi, nblk - 1)
        k_blk = k_ref[0, pl.ds(pl.multiple_of(j * T, T), T), :]
        return [jnp.dot(k_blk[:, i * dv:(i + 1) * dv], wq_ref[i, qi], preferred_element_type=F32)
                for i in range(hpg)]

    for i, t in enumerate(scores(0, 0)):
        t_ref[0, i] = t

    def step(s, carry):
        qi, j, qi_prev, j_prev = carry
        cur = jnp.bitwise_and(s, 1)
        prev = 1 - cur
        is_diag = j == qi
        qi_next = jnp.where(is_diag, qi + 1, qi)
        j_next = jnp.where(is_diag, 0, j + 1)

        t_next = scores(qi_next, j_next)

        for i in range(hpg):
            pv = jnp.dot(vt_ref[i, j_prev], p_ref[prev, i], preferred_element_type=F32)
            acc_ref[i] = alpha_ref[i] * acc_ref[i] + pv

        parity = jnp.bitwise_and(qi, 1)
        block_off = ((j - qi) * T).astype(F32)
        for i in range(hpg):
            t = t_ref[cur, i] + bias_ref[i, is_diag.astype(jnp.int32)]
            m_old = jnp.where(j == 0, NEG, m_ref[parity, i])
            l_old = jnp.where(j == 0, 0.0, l_ref[parity, i])
            shift = slope_ref[i] * block_off
            m_new = jnp.maximum(m_old, jnp.max(t, axis=0, keepdims=True) + shift)
            alpha = jnp.exp2(m_old - m_new)
            p = jnp.exp2(t + (shift - m_new))
            m_ref[parity, i] = m_new
            l_ref[parity, i] = alpha * l_old + jnp.sum(p, axis=0, keepdims=True)
            alpha_ref[i] = alpha
            p_ref[cur, i] = p.astype(BF16)

        for i in range(hpg):
            t_ref[prev, i] = t_next[i]

        @pl.when(j_prev == qi_prev)
        def _():
            q0 = pl.multiple_of(qi_prev * T, T)
            parity_prev = jnp.bitwise_and(qi_prev, 1)
            for i in range(hpg):
                acc = acc_ref[i]
                r = 1.0 / l_ref[parity_prev, i]
                out = acc[:, :T] * r[:, :T] - lam * (acc[:, T:] * r[:, T:])
                ms = jnp.mean(out * out, axis=0, keepdims=True)
                out = out * lax.rsqrt(ms + EPS) * subln_ref[...] * (1.0 - lam_init)
                o_ref[0, pl.ds(q0, T), i * dv:(i + 1) * dv] = out.T.astype(BF16)

        return qi_next, j_next, qi, j

    lax.fori_loop(0, nsteps + 1, step, (jnp.int32(0), jnp.int32(0), jnp.int32(1), jnp.int32(0)))


def _diff_attention(dq, dk, dv, diff_lambda, subln, *, heads, lam_init, T=256, hpg=1):
    B, S, W = dq.shape
    hw = W // heads
    dh = hw // 2
    slopes = jnp.exp2(-8.0 * (jnp.arange(heads, dtype=F32) + 1.0) / heads) * LOG2E
    group_spec = pl.BlockSpec((1, S, hpg * hw), lambda b, g: (b, 0, g))
    return pl.pallas_call(
        functools.partial(_diff_kernel, T=T, seq=S, dh=dh, hpg=hpg, lam_init=lam_init),
        out_shape=jax.ShapeDtypeStruct((B, S, W), BF16),
        grid=(B, heads // hpg),
        in_specs=[group_spec, group_spec, group_spec,
                  _resident(diff_lambda.shape),
                  _resident((hw, 1)),
                  pl.BlockSpec((hpg, 1, 1), lambda b, g: (g, 0, 0))],
        out_specs=group_spec,
        scratch_shapes=[pltpu.VMEM((hpg, S // T, hw, T), BF16),
                        pltpu.VMEM((hpg, S // T, hw, 2 * T), BF16),
                        pltpu.VMEM((hpg, 2, T, 2 * T), F32),
                        pltpu.VMEM((2, hpg, T, 2 * T), F32),
                        pltpu.VMEM((2, hpg, T, 2 * T), BF16),
                        pltpu.VMEM((hpg, hw, 2 * T), F32),
                        pltpu.VMEM((2, hpg, 1, 2 * T), F32),
                        pltpu.VMEM((2, hpg, 1, 2 * T), F32),
                        pltpu.VMEM((hpg, 1, 2 * T), F32)],
        compiler_params=_params("arbitrary", "arbitrary"),
        name="diffattn",
    )(dq, dk, dv, diff_lambda, subln.reshape(hw, 1), slopes.reshape(heads, 1, 1))


def _mixout_kernel(x_ref, mod_ref, ret_ref, dif_ref, gs_ref, wro_ref, wdo_ref, wo_ref, o_ref):
    D = x_ref.shape[-1]
    y_ret = jnp.dot(ret_ref[...], wro_ref[...], preferred_element_type=F32)
    y_dif = jnp.dot(dif_ref[...], wdo_ref[...], preferred_element_type=F32)
    y = gs_ref[:, :D].astype(F32) * y_ret + gs_ref[:, D:].astype(F32) * y_dif
    gate = mod_ref[0, N_MOD * 1 + 2:N_MOD * 1 + 3, :]
    o_ref[...] = x_ref[...] + gate * jnp.dot(y.astype(BF16), wo_ref[...], preferred_element_type=F32)


def _mixout(x2d, mod3, ret2d, dif2d, gs2d, w_ro, w_do, w_o, *, seq, tm=512):
    M, D = x2d.shape
    tiles_per_seq = seq // tm
    row = lambda w: pl.BlockSpec((tm, w), lambda i: (i, 0))
    return pl.pallas_call(
        _mixout_kernel,
        out_shape=jax.ShapeDtypeStruct((M, D), F32),
        grid=(M // tm,),
        in_specs=[row(D),
                  pl.BlockSpec((1, N_SUB * N_MOD, D), lambda i: (i // tiles_per_seq, 0, 0)),
                  row(ret2d.shape[1]), row(dif2d.shape[1]), row(gs2d.shape[1]),
                  _resident(w_ro.shape), _resident(w_do.shape), _resident(w_o.shape)],
        out_specs=row(D),
        compiler_params=_params("arbitrary"),
        name="mixout",
    )(x2d, mod3, ret2d, dif2d, gs2d, w_ro, w_do, w_o)


def kernel(x, c, w_cond, b_cond, g_norm, w_ffn1_in, w_ffn1_out, w_in, w_ret_out, diff_lambda,
           diff_subln, w_diff_out, w_out, w_ffn2_in, w_ffn2_out, g_final):
    B, S, D = x.shape
    depth = w_cond.shape[0]
    diff_heads = D // (2 * DIFF_HEAD_DIM)
    xf = x.reshape(B * S, D)
    for l in range(depth):
        lam_init = 0.8 - 0.6 * math.exp(-0.3 * l)
        mod3 = _mod(c, w_cond[l], b_cond[l]).reshape(B, N_SUB * N_MOD, D)
        last = l == depth - 1

        xf = _ffn(xf, mod3, g_norm[l], w_ffn1_in[l].astype(BF16), w_ffn1_out[l].astype(BF16), None,
                  sub=0, seq=S)

        rq, rk, rv, sg, dq, dk, dv, gs = _inproj(xf, mod3, g_norm[l], w_in[l].astype(BF16), seq=S)
        b3 = lambda t: t.reshape(B, S, t.shape[-1])
        ret = _retention(b3(rq), b3(rk), b3(rv), b3(sg), heads=RET_HEADS, C=RET_CHUNK)
        dif = _diff_attention(b3(dq), b3(dk), b3(dv), diff_lambda[l], diff_subln[l],
                              heads=diff_heads, lam_init=lam_init)
        xf = _mixout(xf, mod3, ret.reshape(B * S, -1), dif.reshape(B * S, -1), gs,
                     w_ret_out[l].astype(BF16), w_diff_out[l].astype(BF16), w_out[l].astype(BF16), seq=S)

        xf = _ffn(xf, mod3, g_norm[l], w_ffn2_in[l].astype(BF16), w_ffn2_out[l].astype(BF16),
                  g_final if last else None, sub=2, seq=S)
    if depth == 0:
        raise ValueError("depth must be positive")
    return xf.reshape(B, S, D)
```

```python
import functools
import math

import jax
import jax.numpy as jnp
from jax import lax
from jax.experimental import pallas as pl
from jax.experimental.pallas import tpu as pltpu

F32 = jnp.float32
BF16 = jnp.bfloat16
EPS = 1e-6
NEG = -1e30
LOG2E = math.log2(math.e)

RET_HEADS = 4
RET_CHUNK = 256
DIFF_HEAD_DIM = 64
N_SUB = 3
N_MOD = 3

VMEM_LIMIT_BYTES = 56 * 1024 * 1024
BF16_SUBLANES = 16
LANES = 128


def _resident(shape):
    nd = len(shape)
    return pl.BlockSpec(shape, lambda *_: (0,) * nd, pipeline_mode=pl.Buffered(1))


def _params(*sem):
    return pltpu.CompilerParams(dimension_semantics=sem, vmem_limit_bytes=VMEM_LIMIT_BYTES)


def _silu(a):
    return a / (1.0 + jnp.exp(-a))


def _sigmoid(a):
    return 1.0 / (1.0 + jnp.exp(-a))


def _modulated(x, mod_ref, g_ref, sub):
    shift = mod_ref[0, N_MOD * sub:N_MOD * sub + 1, :]
    scale = mod_ref[0, N_MOD * sub + 1:N_MOD * sub + 2, :]
    ms = jnp.mean(x * x, axis=-1, keepdims=True)
    h = x * lax.rsqrt(ms + EPS) * g_ref[sub:sub + 1, :]
    return (h * (1.0 + scale) + shift).astype(BF16)


def _mod_kernel(c_ref, w_ref, b_ref, o_ref):
    c_act = _silu(c_ref[...]).astype(BF16)
    o_ref[...] = jnp.dot(c_act, w_ref[...].astype(BF16), preferred_element_type=F32) + b_ref[...]


def _mod(c, w, b):
    B, D = c.shape
    N = w.shape[1]
    tn = 1024
    return pl.pallas_call(
        _mod_kernel,
        out_shape=jax.ShapeDtypeStruct((B, N), F32),
        grid=(N // tn,),
        in_specs=[pl.BlockSpec((B, D), lambda j: (0, 0)),
                  pl.BlockSpec((D, tn), lambda j: (0, j)),
                  pl.BlockSpec((1, tn), lambda j: (0, j))],
        out_specs=pl.BlockSpec((B, tn), lambda j: (0, j)),
        compiler_params=_params("arbitrary"),
        name="mod",
    )(c, w, b.reshape(1, N))


def _ffn_kernel(x_ref, mod_ref, g_ref, win_ref, wout_ref, *rest, sub, d_ff, tf, final):
    if final:
        gfin_ref, o_ref, act_ref = rest
    else:
        o_ref, act_ref = rest
    x = x_ref[...]
    h = _modulated(x, mod_ref, g_ref, sub)
    for c in range(d_ff // tf):
        a = jnp.dot(h, win_ref[:, c * tf:(c + 1) * tf], preferred_element_type=F32)
        b = jnp.dot(h, win_ref[:, d_ff + c * tf:d_ff + (c + 1) * tf], preferred_element_type=F32)
        act_ref[:, c * tf:(c + 1) * tf] = (_silu(a) * b).astype(BF16)
    y = jnp.dot(act_ref[...], wout_ref[...], preferred_element_type=F32)
    gate = mod_ref[0, N_MOD * sub + 2:N_MOD * sub + 3, :]
    out = x + (0.5 * gate) * y
    if final:
        ms = jnp.mean(out * out, axis=-1, keepdims=True)
        out = out * lax.rsqrt(ms + EPS) * gfin_ref[...]
    o_ref[...] = out


def _ffn(x2d, mod3, g_norm, w_in, w_out, g_final, *, sub, seq, tm=512, tf=256):
    M, D = x2d.shape
    d_ff = w_out.shape[0]
    tiles_per_seq = seq // tm
    final = g_final is not None
    in_specs = [pl.BlockSpec((tm, D), lambda i: (i, 0)),
                pl.BlockSpec((1, N_SUB * N_MOD, D), lambda i: (i // tiles_per_seq, 0, 0)),
                _resident(g_norm.shape),
                _resident(w_in.shape),
                _resident(w_out.shape)]
    args = [x2d, mod3, g_norm, w_in, w_out]
    if final:
        in_specs.append(_resident((1, D)))
        args.append(g_final.reshape(1, D))
    return pl.pallas_call(
        functools.partial(_ffn_kernel, sub=sub, d_ff=d_ff, tf=tf, final=final),
        out_shape=jax.ShapeDtypeStruct((M, D), F32),
        grid=(M // tm,),
        in_specs=in_specs,
        out_specs=pl.BlockSpec((tm, D), lambda i: (i, 0)),
        scratch_shapes=[pltpu.VMEM((tm, d_ff), BF16)],
        compiler_params=_params("arbitrary"),
        name="ffn%d" % sub,
    )(*args)


def _inproj_segments(D):
    rqk, rv = D, 2 * D
    return (("rq", rqk, None), ("rk", rqk, "ret_k_scale"), ("rv", rv, None), ("rg", rv, "silu"),
            ("dq", D, "diff_q_scale"), ("dk", D, None), ("dv", D, None), ("gates", 2 * D, "sigmoid"))


def _inproj_kernel(x_ref, mod_ref, g_ref, w_ref, *out_refs, segs, tn, ret_k_scale, diff_q_scale):
    h = _modulated(x_ref[...], mod_ref, g_ref, 1)
    col = 0
    for (_, width, epi), o_ref in zip(segs, out_refs):
        for c in range(width // tn):
            y = jnp.dot(h, w_ref[:, col + c * tn:col + (c + 1) * tn], preferred_element_type=F32)
            if epi == "silu":
                y = _silu(y)
            elif epi == "sigmoid":
                y = _sigmoid(y)
            elif epi == "ret_k_scale":
                y = y * ret_k_scale
            elif epi == "diff_q_scale":
                y = y * diff_q_scale
            o_ref[:, c * tn:(c + 1) * tn] = y.astype(BF16)
        col += width


def _inproj(x2d, mod3, g_norm, w_in, *, seq, tm=256, tn=512):
    M, D = x2d.shape
    segs = _inproj_segments(D)
    assert sum(s[1] for s in segs) == w_in.shape[1]
    tiles_per_seq = seq // tm
    kern = functools.partial(
        _inproj_kernel, segs=segs, tn=tn,
        ret_k_scale=float((D // RET_HEADS) ** -0.5),
        diff_q_scale=LOG2E * DIFF_HEAD_DIM ** -0.5)
    return pl.pallas_call(
        kern,
        out_shape=[jax.ShapeDtypeStruct((M, s[1]), BF16) for s in segs],
        grid=(M // tm,),
        in_specs=[pl.BlockSpec((tm, D), lambda i: (i, 0)),
                  pl.BlockSpec((1, N_SUB * N_MOD, D), lambda i: (i // tiles_per_seq, 0, 0)),
                  _resident(g_norm.shape),
                  _resident(w_in.shape)],
        out_specs=[pl.BlockSpec((tm, s[1]), lambda i: (i, 0)) for s in segs],
        compiler_params=_params("arbitrary"),
        name="inproj",
    )(x2d, mod3, g_norm, w_in)


def _ret_kernel(q_ref, k_ref, v_ref, sg_ref, dec_ref, xi_ref, zeta_ref, gch_ref, o_ref, state_ref,
                *, heads, dk, dv):
    @pl.when(pl.program_id(1) == 0)
    def _():
        state_ref[...] = jnp.zeros_like(state_ref)

    C = q_ref.shape[1]
    hs = range(heads)
    q = [q_ref[0, :, h * dk:(h + 1) * dk] for h in hs]
    k = [k_ref[0, :, h * dk:(h + 1) * dk] for h in hs]
    v = [v_ref[0, :, h * dv:(h + 1) * dv] for h in hs]
    def first_matmuls(h):
        s = lax.dot_general(q[h], k[h], (((1,), (1,)), ((), ())), preferred_element_type=F32)
        cross = jnp.dot(q[h], state_ref[h].astype(BF16), preferred_element_type=F32)
        return s, cross

    def stacked_lhs(h, s):
        intra = (s * dec_ref[h]).astype(BF16)
        kz_t = (k[h].astype(F32) * zeta_ref[h]).T.astype(BF16)
        return jnp.concatenate([intra, kz_t], axis=0)

    def finish(h, both, cross):
        state_ref[h] = state_ref[h] * gch_ref[h] + both[C:, :]
        o = both[:C, :] + cross * xi_ref[h]
        mu = jnp.mean(o, axis=-1, keepdims=True)
        d = o - mu
        var = jnp.mean(d * d, axis=-1, keepdims=True)
        on = d * lax.rsqrt(var + EPS)
        o_ref[0, :, h * dv:(h + 1) * dv] = (on * sg_ref[0, :, h * dv:(h + 1) * dv].astype(F32)).astype(BF16)

    for h0 in range(0, heads, 2):
        pair = range(h0, min(h0 + 2, heads))
        first = [first_matmuls(h) for h in pair]
        both = [jnp.dot(stacked_lhs(h, s), v[h], preferred_element_type=F32)
                for h, (s, _) in zip(pair, first)]
        for h, b, (_, cross) in zip(pair, both, first):
            finish(h, b, cross)


def _retention_tables(heads, C):
    hh = jnp.arange(heads, dtype=F32)
    log_g = jnp.log1p(-jnp.exp2(-5.0 - hh))
    pos = jnp.arange(C, dtype=F32)
    rel = pos[:, None] - pos[None, :]
    decay = jnp.where(rel[None] >= 0, jnp.exp(jnp.maximum(rel, 0.0)[None] * log_g[:, None, None]), 0.0)
    xi = jnp.exp((pos + 1.0)[None, :] * log_g[:, None])[..., None]
    zeta = jnp.exp((C - 1.0 - pos)[None, :] * log_g[:, None])[..., None]
    g_chunk = jnp.exp(C * log_g)[:, None, None]
    return decay, xi, zeta, g_chunk


def _retention(rq, rk, rv, sg, *, heads, C):
    B, S, qk_w = rq.shape
    v_w = rv.shape[-1]
    dk, dv = qk_w // heads, v_w // heads
    decay, xi, zeta, g_chunk = _retention_tables(heads, C)
    qk_spec = pl.BlockSpec((1, C, qk_w), lambda b, n: (b, n, 0))
    v_spec = pl.BlockSpec((1, C, v_w), lambda b, n: (b, n, 0))
    return pl.pallas_call(
        functools.partial(_ret_kernel, heads=heads, dk=dk, dv=dv),
        out_shape=jax.ShapeDtypeStruct((B, S, v_w), BF16),
        grid=(B, S // C),
        in_specs=[qk_spec, qk_spec, v_spec, v_spec,
                  _resident(decay.shape), _resident(xi.shape), _resident(zeta.shape),
                  _resident(g_chunk.shape)],
        out_specs=v_spec,
        scratch_shapes=[pltpu.VMEM((heads, dk, dv), F32)],
        compiler_params=_params("arbitrary", "arbitrary"),
        name="retention",
    )(rq, rk, rv, sg, decay, xi, zeta, g_chunk)


STEPS_PER_ITER = 4


def _diff_kernel(q_ref, k_ref, v_ref, lam_ref, subln_ref, slope_ref, o_ref,
                 vt_ref, wq_ref, kaug_ref, t_a_ref, t_b_ref, p_a_ref, p_b_ref, acc_ref, m_ref, alpha_ref,
                 *, T, seq, dh, lam_init):
    nblk = seq // T
    n_off = nblk * (nblk - 1) // 2
    dv = 2 * dh
    dv_ext = dv + BF16_SUBLANES
    dummy = nblk
    lp = lam_ref[...]
    lam = (jnp.exp(jnp.sum(lp[0:1] * lp[1:2], axis=-1, keepdims=True))
           - jnp.exp(jnp.sum(lp[2:3] * lp[3:4], axis=-1, keepdims=True)) + lam_init)

    row = lax.broadcasted_iota(jnp.int32, (T, LANES), 0)
    lane = lax.broadcasted_iota(jnp.int32, (T, LANES), 1)
    kaug_ref[...] = jnp.where(lane < 3, row, 0).astype(F32).astype(BF16)

    slope = slope_ref[0]
    s_hi = slope.astype(BF16).astype(F32)
    s_mid = (slope - s_hi).astype(BF16).astype(F32)
    s_lo = slope - s_hi - s_mid
    aug_row = lax.broadcasted_iota(jnp.int32, (LANES, 2 * T), 0)
    slope_rows = jnp.where(aug_row == 0, s_hi, jnp.where(aug_row == 1, s_mid,
                           jnp.where(aug_row == 2, s_lo, 0.0))).astype(BF16)
    ones_rows = jnp.where(lax.broadcasted_iota(jnp.int32, (BF16_SUBLANES, T), 0) == 0, 1.0, 0.0).astype(BF16)
    feat_id = lax.broadcasted_iota(jnp.int32, (dv, T), 0)
    for j in range(nblk):
        rows = slice(j * T, (j + 1) * T)
        vt_ref[j, 0:dv, :] = v_ref[0, rows, :].astype(F32).T.astype(BF16)
        vt_ref[j, dv:dv_ext, :] = ones_rows
        qt = q_ref[0, rows, :].astype(F32).T
        wq_ref[j, 0:dv, :] = jnp.concatenate([jnp.where(feat_id < dh, qt, 0.0),
                                              jnp.where(feat_id >= dh, qt, 0.0)], axis=1).astype(BF16)
        wq_ref[j, dv:dv + LANES, :] = slope_rows

    def scores(qi, j):
        k0 = j * T if isinstance(j, int) else pl.multiple_of(j * T, T)
        k_blk = k_ref[0, pl.ds(k0, T), :]
        return jnp.dot(jnp.concatenate([k_blk, kaug_ref[...]], axis=1), wq_ref[qi],
                       preferred_element_type=F32)

    key_id = lax.broadcasted_iota(jnp.int32, (T, 2 * T), 0)
    col_id = lax.broadcasted_iota(jnp.int32, (T, 2 * T), 1)
    causal = key_id <= jnp.where(col_id >= T, col_id - T, col_id)

    m_ref[dummy] = jnp.zeros((1, 2 * T), F32)
    acc_ref[nblk - 1] = jnp.zeros((dv_ext, 2 * T), F32)
    p_b_ref[...] = jnp.zeros((T, 2 * T), BF16)
    t_a_ref[...] = scores(0, 0)

    def next_block(qi, j):
        on_diag = j == qi
        last_diag = qi == nblk - 1
        end_of_tile = j == qi - 1
        qi_next = jnp.where(on_diag, jnp.where(last_diag, 1, qi + 1), jnp.where(end_of_tile, qi + 1, qi))
        j_next = jnp.where(on_diag, jnp.where(last_diag, 0, qi + 1), jnp.where(end_of_tile, 0, j + 1))
        return qi_next, j_next

    def step(carry, t_cur_ref, t_nxt_ref, p_cur_ref, p_prv_ref, masked):
        qi, j, qi_prev, j_prev = carry
        qi_next, j_next = next_block(qi, j)

        t_next = scores(jnp.minimum(qi_next, nblk - 1), j_next)

        pv = jnp.dot(vt_ref[j_prev], p_prv_ref[...], preferred_element_type=F32)
        if masked:
            acc_ref[qi_prev] = pv
        else:
            acc_ref[qi_prev] = alpha_ref[...] * acc_ref[qi_prev] + pv

        t = t_cur_ref[...]
        if masked:
            t = jnp.where(causal, t, NEG)
            m_new = jnp.max(t, axis=0, keepdims=True)
            alpha_ref[...] = jnp.zeros_like(alpha_ref)
            p_cur_ref[...] = jnp.exp2(t - m_new).astype(BF16)
        else:
            m_old = m_ref[qi]
            shift = slope * ((j - qi) * T).astype(F32)
            m_new = jnp.maximum(m_old, jnp.max(t, axis=0, keepdims=True) + shift)
            alpha_ref[...] = jnp.exp2(m_old - m_new)
            p_cur_ref[...] = jnp.exp2(t + (shift - m_new)).astype(BF16)
        m_ref[qi] = m_new

        t_nxt_ref[...] = t_next
        return qi_next, j_next, qi, j

    def steps(masked, _, carry):
        for _ in range(STEPS_PER_ITER // 2):
            carry = step(carry, t_a_ref, t_b_ref, p_a_ref, p_b_ref, masked)
            carry = step(carry, t_b_ref, t_a_ref, p_b_ref, p_a_ref, masked)
        return carry

    carry = (jnp.int32(0), jnp.int32(0), jnp.int32(dummy), jnp.int32(0))
    carry = lax.fori_loop(0, nblk // STEPS_PER_ITER, functools.partial(steps, True), carry)
    lax.fori_loop(0, pl.cdiv(n_off + 1, STEPS_PER_ITER), functools.partial(steps, False), carry)

    gain = jnp.broadcast_to(subln_ref[...] * (1.0 - lam_init), (dv, T))
    for qi in range(nblk):
        acc = acc_ref[qi]
        r = 1.0 / acc[dv:dv + 1, :]
        out = acc[0:dv, :T] * r[:, :T] - acc[0:dv, T:] * (lam * r[:, T:])
        ms = jnp.mean(out * out, axis=0, keepdims=True)
        o_ref[0, qi * T:(qi + 1) * T, :] = (out * lax.rsqrt(ms + EPS) * gain).T.astype(BF16)


def _diff_attention(dq, dk, dv, diff_lambda, subln, *, heads, lam_init, T=256):
    B, S, W = dq.shape
    hw = W // heads
    dh = hw // 2
    nblk = S // T
    assert T <= 256 and hw == LANES
    assert nblk % STEPS_PER_ITER == 0 and STEPS_PER_ITER % 2 == 0
    slopes = jnp.exp2(-8.0 * (jnp.arange(heads, dtype=F32) + 1.0) / heads) * LOG2E
    head_spec = pl.BlockSpec((1, S, hw), lambda b, h: (b, 0, h))
    return pl.pallas_call(
        functools.partial(_diff_kernel, T=T, seq=S, dh=dh, lam_init=lam_init),
        out_shape=jax.ShapeDtypeStruct((B, S, W), BF16),
        grid=(B, heads),
        in_specs=[head_spec, head_spec, head_spec,
                  _resident(diff_lambda.shape),
                  _resident((hw, 1)),
                  pl.BlockSpec((1, 1, 1), lambda b, h: (h, 0, 0))],
        out_specs=head_spec,
        scratch_shapes=[pltpu.VMEM((nblk, hw + BF16_SUBLANES, T), BF16),
                        pltpu.VMEM((nblk, hw + LANES, 2 * T), BF16),
                        pltpu.VMEM((T, LANES), BF16),
                        pltpu.VMEM((T, 2 * T), F32), pltpu.VMEM((T, 2 * T), F32),
                        pltpu.VMEM((T, 2 * T), BF16), pltpu.VMEM((T, 2 * T), BF16),
                        pltpu.VMEM((nblk + 1, hw + BF16_SUBLANES, 2 * T), F32),
                        pltpu.VMEM((nblk + 1, 1, 2 * T), F32),
                        pltpu.VMEM((1, 2 * T), F32)],
        compiler_params=_params("arbitrary", "arbitrary"),
        name="diffattn",
    )(dq, dk, dv, diff_lambda, subln.reshape(hw, 1), slopes.reshape(heads, 1, 1))


def _mixout_kernel(x_ref, mod_ref, ret_ref, dif_ref, gs_ref, wro_ref, wdo_ref, wo_ref, o_ref):
    D = x_ref.shape[-1]
    y_ret = jnp.dot(ret_ref[...], wro_ref[...], preferred_element_type=F32)
    y_dif = jnp.dot(dif_ref[...], wdo_ref[...], preferred_element_type=F32)
    y = gs_ref[:, :D].astype(F32) * y_ret + gs_ref[:, D:].astype(F32) * y_dif
    gate = mod_ref[0, N_MOD * 1 + 2:N_MOD * 1 + 3, :]
    o_ref[...] = x_ref[...] + gate * jnp.dot(y.astype(BF16), wo_ref[...], preferred_element_type=F32)


def _mixout(x2d, mod3, ret2d, dif2d, gs2d, w_ro, w_do, w_o, *, seq, tm=512):
    M, D = x2d.shape
    tiles_per_seq = seq // tm
    row = lambda w: pl.BlockSpec((tm, w), lambda i: (i, 0))
    return pl.pallas_call(
        _mixout_kernel,
        out_shape=jax.ShapeDtypeStruct((M, D), F32),
        grid=(M // tm,),
        in_specs=[row(D),
                  pl.BlockSpec((1, N_SUB * N_MOD, D), lambda i: (i // tiles_per_seq, 0, 0)),
                  row(ret2d.shape[1]), row(dif2d.shape[1]), row(gs2d.shape[1]),
                  _resident(w_ro.shape), _resident(w_do.shape), _resident(w_o.shape)],
        out_specs=row(D),
        compiler_params=_params("arbitrary"),
        name="mixout",
    )(x2d, mod3, ret2d, dif2d, gs2d, w_ro, w_do, w_o)


def kernel(x, c, w_cond, b_cond, g_norm, w_ffn1_in, w_ffn1_out, w_in, w_ret_out, diff_lambda,
           diff_subln, w_diff_out, w_out, w_ffn2_in, w_ffn2_out, g_final):
    B, S, D = x.shape
    depth = w_cond.shape[0]
    diff_heads = D // (2 * DIFF_HEAD_DIM)
    xf = x.reshape(B * S, D)
    for l in range(depth):
        lam_init = 0.8 - 0.6 * math.exp(-0.3 * l)
        mod3 = _mod(c, w_cond[l], b_cond[l]).reshape(B, N_SUB * N_MOD, D)
        last = l == depth - 1

        xf = _ffn(xf, mod3, g_norm[l], w_ffn1_in[l].astype(BF16), w_ffn1_out[l].astype(BF16), None,
                  sub=0, seq=S)

        rq, rk, rv, sg, dq, dk, dv, gs = _inproj(xf, mod3, g_norm[l], w_in[l].astype(BF16), seq=S)
        b3 = lambda t: t.reshape(B, S, t.shape[-1])
        ret = _retention(b3(rq), b3(rk), b3(rv), b3(sg), heads=RET_HEADS, C=RET_CHUNK)
        dif = _diff_attention(b3(dq), b3(dk), b3(dv), diff_lambda[l], diff_subln[l],
                              heads=diff_heads, lam_init=lam_init)
        xf = _mixout(xf, mod3, ret.reshape(B * S, -1), dif.reshape(B * S, -1), gs,
                     w_ret_out[l].astype(BF16), w_diff_out[l].astype(BF16), w_out[l].astype(BF16), seq=S)

        xf = _ffn(xf, mod3, g_norm[l], w_ffn2_in[l].astype(BF16), w_ffn2_out[l].astype(BF16),
                  g_final if last else None, sub=2, seq=S)
    if depth == 0:
        raise ValueError("depth must be positive")
    return xf.reshape(B, S, D)
```

```python
import functools
import math

import jax
import jax.numpy as jnp
from jax import lax
from jax.experimental import pallas as pl
from jax.experimental.pallas import tpu as pltpu

F32 = jnp.float32
BF16 = jnp.bfloat16
EPS = 1e-6
NEG = -1e30
LOG2E = math.log2(math.e)

RET_HEADS = 4
RET_CHUNK = 256
DIFF_HEAD_DIM = 64
N_SUB = 3
N_MOD = 3

VMEM_LIMIT_BYTES = 56 * 1024 * 1024
BF16_SUBLANES = 16
LANES = 128


def _resident(shape):
    nd = len(shape)
    return pl.BlockSpec(shape, lambda *_: (0,) * nd, pipeline_mode=pl.Buffered(1))


def _params(*sem):
    return pltpu.CompilerParams(dimension_semantics=sem, vmem_limit_bytes=VMEM_LIMIT_BYTES)


def _silu(a):
    return a / (1.0 + jnp.exp(-a))


def _sigmoid(a):
    return 1.0 / (1.0 + jnp.exp(-a))


def _modulated(x, mod_ref, g_ref, sub):
    shift = mod_ref[0, N_MOD * sub:N_MOD * sub + 1, :]
    scale = mod_ref[0, N_MOD * sub + 1:N_MOD * sub + 2, :]
    ms = jnp.mean(x * x, axis=-1, keepdims=True)
    h = x * lax.rsqrt(ms + EPS) * g_ref[sub:sub + 1, :]
    return (h * (1.0 + scale) + shift).astype(BF16)


def _mod_kernel(c_ref, w_ref, b_ref, o_ref):
    c_act = _silu(c_ref[...]).astype(BF16)
    o_ref[...] = jnp.dot(c_act, w_ref[...].astype(BF16), preferred_element_type=F32) + b_ref[...]


def _mod(c, w, b):
    B, D = c.shape
    N = w.shape[1]
    tn = 1024
    return pl.pallas_call(
        _mod_kernel,
        out_shape=jax.ShapeDtypeStruct((B, N), F32),
        grid=(N // tn,),
        in_specs=[pl.BlockSpec((B, D), lambda j: (0, 0)),
                  pl.BlockSpec((D, tn), lambda j: (0, j)),
                  pl.BlockSpec((1, tn), lambda j: (0, j))],
        out_specs=pl.BlockSpec((B, tn), lambda j: (0, j)),
        compiler_params=_params("arbitrary"),
        name="mod",
    )(c, w, b.reshape(1, N))


def _ffn_kernel(x_ref, mod_ref, g_ref, win_ref, wout_ref, *rest, sub, d_ff, tf, final):
    if final:
        gfin_ref, o_ref, act_ref = rest
    else:
        o_ref, act_ref = rest
    x = x_ref[...]
    h = _modulated(x, mod_ref, g_ref, sub)
    for c in range(d_ff // tf):
        a = jnp.dot(h, win_ref[:, c * tf:(c + 1) * tf], preferred_element_type=F32)
        b = jnp.dot(h, win_ref[:, d_ff + c * tf:d_ff + (c + 1) * tf], preferred_element_type=F32)
        act_ref[:, c * tf:(c + 1) * tf] = (_silu(a) * b).astype(BF16)
    y = jnp.dot(act_ref[...], wout_ref[...], preferred_element_type=F32)
    gate = mod_ref[0, N_MOD * sub + 2:N_MOD * sub + 3, :]
    out = x + (0.5 * gate) * y
    if final:
        ms = jnp.mean(out * out, axis=-1, keepdims=True)
        out = out * lax.rsqrt(ms + EPS) * gfin_ref[...]
    o_ref[...] = out


def _ffn(x2d, mod3, g_norm, w_in, w_out, g_final, *, sub, seq, tm=512, tf=256):
    M, D = x2d.shape
    d_ff = w_out.shape[0]
    tiles_per_seq = seq // tm
    final = g_final is not None
    in_specs = [pl.BlockSpec((tm, D), lambda i: (i, 0)),
                pl.BlockSpec((1, N_SUB * N_MOD, D), lambda i: (i // tiles_per_seq, 0, 0)),
                _resident(g_norm.shape),
                _resident(w_in.shape),
                _resident(w_out.shape)]
    args = [x2d, mod3, g_norm, w_in, w_out]
    if final:
        in_specs.append(_resident((1, D)))
        args.append(g_final.reshape(1, D))
    return pl.pallas_call(
        functools.partial(_ffn_kernel, sub=sub, d_ff=d_ff, tf=tf, final=final),
        out_shape=jax.ShapeDtypeStruct((M, D), F32),
        grid=(M // tm,),
        in_specs=in_specs,
        out_specs=pl.BlockSpec((tm, D), lambda i: (i, 0)),
        scratch_shapes=[pltpu.VMEM((tm, d_ff), BF16)],
        compiler_params=_params("arbitrary"),
        name="ffn%d" % sub,
    )(*args)


def _inproj_segments(D):
    rqk, rv = D, 2 * D
    return (("rq", rqk, None), ("rk", rqk, "ret_k_scale"), ("rv", rv, None), ("rg", rv, "silu"),
            ("dq", D, "diff_q_scale"), ("dk", D, None), ("dv", D, None), ("gates", 2 * D, "sigmoid"))


def _inproj_kernel(x_ref, mod_ref, g_ref, w_ref, *out_refs, segs, tn, ret_k_scale, diff_q_scale):
    h = _modulated(x_ref[...], mod_ref, g_ref, 1)
    col = 0
    for (_, width, epi), o_ref in zip(segs, out_refs):
        for c in range(width // tn):
            y = jnp.dot(h, w_ref[:, col + c * tn:col + (c + 1) * tn], preferred_element_type=F32)
            if epi == "silu":
                y = _silu(y)
            elif epi == "sigmoid":
                y = _sigmoid(y)
            elif epi == "ret_k_scale":
                y = y * ret_k_scale
            elif epi == "diff_q_scale":
                y = y * diff_q_scale
            o_ref[:, c * tn:(c + 1) * tn] = y.astype(BF16)
        col += width


def _inproj(x2d, mod3, g_norm, w_in, *, seq, tm=256, tn=512):
    M, D = x2d.shape
    segs = _inproj_segments(D)
    assert sum(s[1] for s in segs) == w_in.shape[1]
    tiles_per_seq = seq // tm
    kern = functools.partial(
        _inproj_kernel, segs=segs, tn=tn,
        ret_k_scale=float((D // RET_HEADS) ** -0.5),
        diff_q_scale=LOG2E * DIFF_HEAD_DIM ** -0.5)
    return pl.pallas_call(
        kern,
        out_shape=[jax.ShapeDtypeStruct((M, s[1]), BF16) for s in segs],
        grid=(M // tm,),
        in_specs=[pl.BlockSpec((tm, D), lambda i: (i, 0)),
                  pl.BlockSpec((1, N_SUB * N_MOD, D), lambda i: (i // tiles_per_seq, 0, 0)),
                  _resident(g_norm.shape),
                  _resident(w_in.shape)],
        out_specs=[pl.BlockSpec((tm, s[1]), lambda i: (i, 0)) for s in segs],
        compiler_params=_params("arbitrary"),
        name="inproj",
    )(x2d, mod3, g_norm, w_in)


def _ret_kernel(q_ref, k_ref, v_ref, sg_ref, dec_ref, xi_ref, zeta_ref, gch_ref, o_ref, state_ref,
                *, heads, dk, dv):
    @pl.when(pl.program_id(1) == 0)
    def _():
        state_ref[...] = jnp.zeros_like(state_ref)

    C = q_ref.shape[1]
    hs = range(heads)
    q = [q_ref[0, :, h * dk:(h + 1) * dk] for h in hs]
    k = [k_ref[0, :, h * dk:(h + 1) * dk] for h in hs]
    v = [v_ref[0, :, h * dv:(h + 1) * dv] for h in hs]
    def first_matmuls(h):
        s = lax.dot_general(q[h], k[h], (((1,), (1,)), ((), ())), preferred_element_type=F32)
        cross = jnp.dot(q[h], state_ref[h].astype(BF16), preferred_element_type=F32)
        return s, cross

    def stacked_lhs(h, s):
        intra = (s * dec_ref[h]).astype(BF16)
        kz_t = (k[h].astype(F32) * zeta_ref[h]).T.astype(BF16)
        return jnp.concatenate([intra, kz_t], axis=0)

    def finish(h, both, cross):
        state_ref[h] = state_ref[h] * gch_ref[h] + both[C:, :]
        o = both[:C, :] + cross * xi_ref[h]
        mu = jnp.mean(o, axis=-1, keepdims=True)
        d = o - mu
        var = jnp.mean(d * d, axis=-1, keepdims=True)
        on = d * lax.rsqrt(var + EPS)
        o_ref[0, :, h * dv:(h + 1) * dv] = (on * sg_ref[0, :, h * dv:(h + 1) * dv].astype(F32)).astype(BF16)

    for h0 in range(0, heads, 2):
        pair = range(h0, min(h0 + 2, heads))
        first = [first_matmuls(h) for h in pair]
        both = [jnp.dot(stacked_lhs(h, s), v[h], preferred_element_type=F32)
                for h, (s, _) in zip(pair, first)]
        for h, b, (_, cross) in zip(pair, both, first):
            finish(h, b, cross)


def _retention_tables(heads, C):
    hh = jnp.arange(heads, dtype=F32)
    log_g = jnp.log1p(-jnp.exp2(-5.0 - hh))
    pos = jnp.arange(C, dtype=F32)
    rel = pos[:, None] - pos[None, :]
    decay = jnp.where(rel[None] >= 0, jnp.exp(jnp.maximum(rel, 0.0)[None] * log_g[:, None, None]), 0.0)
    xi = jnp.exp((pos + 1.0)[None, :] * log_g[:, None])[..., None]
    zeta = jnp.exp((C - 1.0 - pos)[None, :] * log_g[:, None])[..., None]
    g_chunk = jnp.exp(C * log_g)[:, None, None]
    return decay, xi, zeta, g_chunk


def _retention(rq, rk, rv, sg, *, heads, C):
    B, S, qk_w = rq.shape
    v_w = rv.shape[-1]
    dk, dv = qk_w // heads, v_w // heads
    decay, xi, zeta, g_chunk = _retention_tables(heads, C)
    qk_spec = pl.BlockSpec((1, C, qk_w), lambda b, n: (b, n, 0))
    v_spec = pl.BlockSpec((1, C, v_w), lambda b, n: (b, n, 0))
    return pl.pallas_call(
        functools.partial(_ret_kernel, heads=heads, dk=dk, dv=dv),
        out_shape=jax.ShapeDtypeStruct((B, S, v_w), BF16),
        grid=(B, S // C),
        in_specs=[qk_spec, qk_spec, v_spec, v_spec,
                  _resident(decay.shape), _resident(xi.shape), _resident(zeta.shape),
                  _resident(g_chunk.shape)],
        out_specs=v_spec,
        scratch_shapes=[pltpu.VMEM((heads, dk, dv), F32)],
        compiler_params=_params("arbitrary", "arbitrary"),
        name="retention",
    )(rq, rk, rv, sg, decay, xi, zeta, g_chunk)


STEPS_PER_ITER = 4


def _diff_kernel(q_ref, k_ref, v_ref, lam_ref, subln_ref, slope_ref, o_ref,
                 vt_ref, wq_ref, kaug_ref, t_a_ref, t_b_ref, tmax_a_ref, tmax_b_ref, p_a_ref, p_b_ref,
                 acc_ref, m_ref, alpha_ref,
                 *, T, seq, dh, lam_init):
    nblk = seq // T
    n_off = nblk * (nblk - 1) // 2
    dv = 2 * dh
    dv_ext = dv + BF16_SUBLANES
    dummy = nblk
    lp = lam_ref[...]
    lam = (jnp.exp(jnp.sum(lp[0:1] * lp[1:2], axis=-1, keepdims=True))
           - jnp.exp(jnp.sum(lp[2:3] * lp[3:4], axis=-1, keepdims=True)) + lam_init)

    row = lax.broadcasted_iota(jnp.int32, (T, LANES), 0)
    lane = lax.broadcasted_iota(jnp.int32, (T, LANES), 1)
    kaug_ref[...] = jnp.where(lane < 3, row, 0).astype(F32).astype(BF16)

    slope = slope_ref[0]
    s_hi = slope.astype(BF16).astype(F32)
    s_mid = (slope - s_hi).astype(BF16).astype(F32)
    s_lo = slope - s_hi - s_mid
    aug_row = lax.broadcasted_iota(jnp.int32, (LANES, 2 * T), 0)
    slope_rows = jnp.where(aug_row == 0, s_hi, jnp.where(aug_row == 1, s_mid,
                           jnp.where(aug_row == 2, s_lo, 0.0))).astype(BF16)
    ones_rows = jnp.where(lax.broadcasted_iota(jnp.int32, (BF16_SUBLANES, T), 0) == 0, 1.0, 0.0).astype(BF16)
    feat_id = lax.broadcasted_iota(jnp.int32, (dv, T), 0)
    for j in range(nblk):
        rows = slice(j * T, (j + 1) * T)
        vt_ref[j, 0:dv, :] = v_ref[0, rows, :].T
        vt_ref[j, dv:dv_ext, :] = ones_rows
        qt = q_ref[0, rows, :].T
        zero = jnp.zeros_like(qt)
        wq_ref[j, 0:dv, :] = jnp.concatenate([jnp.where(feat_id < dh, qt, zero),
                                              jnp.where(feat_id >= dh, qt, zero)], axis=1)
        wq_ref[j, dv:dv + LANES, :] = slope_rows

    def scores(qi, j):
        k0 = j * T if isinstance(j, int) else pl.multiple_of(j * T, T)
        k_blk = k_ref[0, pl.ds(k0, T), :]
        return jnp.dot(jnp.concatenate([k_blk, kaug_ref[...]], axis=1), wq_ref[qi],
                       preferred_element_type=F32)

    key_id = lax.broadcasted_iota(jnp.int32, (T, 2 * T), 0)
    col_id = lax.broadcasted_iota(jnp.int32, (T, 2 * T), 1)
    causal = key_id <= jnp.where(col_id >= T, col_id - T, col_id)

    acc_ref[nblk - 1] = jnp.zeros((dv_ext, 2 * T), F32)
    p_b_ref[...] = jnp.zeros((T, 2 * T), BF16)

    def store_scores(t_ref, tmax_ref, t, masked):
        if masked:
            t = jnp.where(causal, t, NEG)
        t_ref[...] = t
        tmax_ref[...] = jnp.max(t, axis=0, keepdims=True)

    store_scores(t_a_ref, tmax_a_ref, scores(0, 0), True)

    def next_block(qi, j):
        on_diag = j == qi
        last_diag = qi == nblk - 1
        end_of_tile = j == qi - 1
        qi_next = jnp.where(on_diag, jnp.where(last_diag, 1, qi + 1), jnp.where(end_of_tile, qi + 1, qi))
        j_next = jnp.where(on_diag, jnp.where(last_diag, 0, qi + 1), jnp.where(end_of_tile, 0, j + 1))
        return qi_next, j_next

    buf_a = (t_a_ref, tmax_a_ref, p_a_ref)
    buf_b = (t_b_ref, tmax_b_ref, p_b_ref)

    def step(carry, cur, other, masked, next_masked):
        t_cur_ref, tmax_cur_ref, p_cur_ref = cur
        t_nxt_ref, tmax_nxt_ref, p_prv_ref = other
        qi, j, qi_prev, j_prev = carry
        qi_next, j_next = next_block(qi, j)

        t_next = scores(jnp.minimum(qi_next, nblk - 1), j_next)

        pv = jnp.dot(vt_ref[j_prev], p_prv_ref[...], preferred_element_type=F32)
        if masked:
            acc_ref[qi_prev] = pv
        else:
            acc_ref[qi_prev] = alpha_ref[...] * acc_ref[qi_prev] + pv

        t = t_cur_ref[...]
        if masked:
            m_new = tmax_cur_ref[...]
            alpha_ref[...] = jnp.zeros_like(alpha_ref)
            p_cur_ref[...] = jnp.exp2(t - m_new).astype(BF16)
        else:
            m_old = m_ref[qi]
            shift = slope * ((j - qi) * T).astype(F32)
            m_new = jnp.maximum(m_old, tmax_cur_ref[...] + shift)
            alpha_ref[...] = jnp.exp2(m_old - m_new)
            p_cur_ref[...] = jnp.exp2(t + (shift - m_new)).astype(BF16)
        m_ref[qi] = m_new

        store_scores(t_nxt_ref, tmax_nxt_ref, t_next, next_masked)
        return qi_next, j_next, qi, j

    def steps(masked, last_next_masked, _, carry):
        for u in range(0, STEPS_PER_ITER, 2):
            carry = step(carry, buf_a, buf_b, masked, masked)
            carry = step(carry, buf_b, buf_a, masked,
                         masked if u + 2 < STEPS_PER_ITER else last_next_masked)
        return carry

    carry = (jnp.int32(0), jnp.int32(0), jnp.int32(dummy), jnp.int32(0))
    carry = lax.fori_loop(0, nblk // STEPS_PER_ITER - 1, functools.partial(steps, True, True), carry)
    carry = steps(True, False, 0, carry)
    carry = lax.fori_loop(0, n_off // STEPS_PER_ITER, functools.partial(steps, False, False), carry)
    _, _, qi_last, j_last = carry
    acc_ref[qi_last] = (alpha_ref[...] * acc_ref[qi_last]
                        + jnp.dot(vt_ref[j_last], p_b_ref[...], preferred_element_type=F32))

    gain = jnp.broadcast_to(subln_ref[...] * (1.0 - lam_init), (dv, T))
    for qi in range(nblk):
        acc = acc_ref[qi]
        r = 1.0 / acc[dv:dv + 1, :]
        out = acc[0:dv, :T] * r[:, :T] - acc[0:dv, T:] * (lam * r[:, T:])
        ms = jnp.mean(out * out, axis=0, keepdims=True)
        o_ref[0, qi * T:(qi + 1) * T, :] = (out * lax.rsqrt(ms + EPS) * gain).T.astype(BF16)


def _diff_attention(dq, dk, dv, diff_lambda, subln, *, heads, lam_init, T=256):
    B, S, W = dq.shape
    hw = W // heads
    dh = hw // 2
    nblk = S // T
    assert T <= 256 and hw == LANES
    assert STEPS_PER_ITER % 2 == 0 and nblk % STEPS_PER_ITER == 0
    assert (nblk * (nblk - 1) // 2) % STEPS_PER_ITER == 0
    slopes = jnp.exp2(-8.0 * (jnp.arange(heads, dtype=F32) + 1.0) / heads) * LOG2E
    head_spec = pl.BlockSpec((1, S, hw), lambda b, h: (b, 0, h))
    return pl.pallas_call(
        functools.partial(_diff_kernel, T=T, seq=S, dh=dh, lam_init=lam_init),
        out_shape=jax.ShapeDtypeStruct((B, S, W), BF16),
        grid=(B, heads),
        in_specs=[head_spec, head_spec, head_spec,
                  _resident(diff_lambda.shape),
                  _resident((hw, 1)),
                  pl.BlockSpec((1, 1, 1), lambda b, h: (h, 0, 0))],
        out_specs=head_spec,
        scratch_shapes=[pltpu.VMEM((nblk, hw + BF16_SUBLANES, T), BF16),
                        pltpu.VMEM((nblk, hw + LANES, 2 * T), BF16),
                        pltpu.VMEM((T, LANES), BF16),
                        pltpu.VMEM((T, 2 * T), F32), pltpu.VMEM((T, 2 * T), F32),
                        pltpu.VMEM((1, 2 * T), F32), pltpu.VMEM((1, 2 * T), F32),
                        pltpu.VMEM((T, 2 * T), BF16), pltpu.VMEM((T, 2 * T), BF16),
                        pltpu.VMEM((nblk + 1, hw + BF16_SUBLANES, 2 * T), F32),
                        pltpu.VMEM((nblk, 1, 2 * T), F32),
                        pltpu.VMEM((1, 2 * T), F32)],
        compiler_params=_params("arbitrary", "arbitrary"),
        name="diffattn",
    )(dq, dk, dv, diff_lambda, subln.reshape(hw, 1), slopes.reshape(heads, 1, 1))


def _mixout_kernel(x_ref, mod_ref, ret_ref, dif_ref, gs_ref, wro_ref, wdo_ref, wo_ref, o_ref):
    D = x_ref.shape[-1]
    y_ret = jnp.dot(ret_ref[...], wro_ref[...], preferred_element_type=F32)
    y_dif = jnp.dot(dif_ref[...], wdo_ref[...], preferred_element_type=F32)
    y = gs_ref[:, :D].astype(F32) * y_ret + gs_ref[:, D:].astype(F32) * y_dif
    gate = mod_ref[0, N_MOD * 1 + 2:N_MOD * 1 + 3, :]
    o_ref[...] = x_ref[...] + gate * jnp.dot(y.astype(BF16), wo_ref[...], preferred_element_type=F32)


def _mixout(x2d, mod3, ret2d, dif2d, gs2d, w_ro, w_do, w_o, *, seq, tm=512):
    M, D = x2d.shape
    tiles_per_seq = seq // tm
    row = lambda w: pl.BlockSpec((tm, w), lambda i: (i, 0))
    return pl.pallas_call(
        _mixout_kernel,
        out_shape=jax.ShapeDtypeStruct((M, D), F32),
        grid=(M // tm,),
        in_specs=[row(D),
                  pl.BlockSpec((1, N_SUB * N_MOD, D), lambda i: (i // tiles_per_seq, 0, 0)),
                  row(ret2d.shape[1]), row(dif2d.shape[1]), row(gs2d.shape[1]),
                  _resident(w_ro.shape), _resident(w_do.shape), _resident(w_o.shape)],
        out_specs=row(D),
        compiler_params=_params("arbitrary"),
        name="mixout",
    )(x2d, mod3, ret2d, dif2d, gs2d, w_ro, w_do, w_o)


def kernel(x, c, w_cond, b_cond, g_norm, w_ffn1_in, w_ffn1_out, w_in, w_ret_out, diff_lambda,
           diff_subln, w_diff_out, w_out, w_ffn2_in, w_ffn2_out, g_final):
    B, S, D = x.shape
    depth = w_cond.shape[0]
    diff_heads = D // (2 * DIFF_HEAD_DIM)
    xf = x.reshape(B * S, D)
    for l in range(depth):
        lam_init = 0.8 - 0.6 * math.exp(-0.3 * l)
        mod3 = _mod(c, w_cond[l], b_cond[l]).reshape(B, N_SUB * N_MOD, D)
        last = l == depth - 1

        xf = _ffn(xf, mod3, g_norm[l], w_ffn1_in[l].astype(BF16), w_ffn1_out[l].astype(BF16), None,
                  sub=0, seq=S)

        rq, rk, rv, sg, dq, dk, dv, gs = _inproj(xf, mod3, g_norm[l], w_in[l].astype(BF16), seq=S)
        b3 = lambda t: t.reshape(B, S, t.shape[-1])
        ret = _retention(b3(rq), b3(rk), b3(rv), b3(sg), heads=RET_HEADS, C=RET_CHUNK)
        dif = _diff_attention(b3(dq), b3(dk), b3(dv), diff_lambda[l], diff_subln[l],
                              heads=diff_heads, lam_init=lam_init)
        xf = _mixout(xf, mod3, ret.reshape(B * S, -1), dif.reshape(B * S, -1), gs,
                     w_ret_out[l].astype(BF16), w_diff_out[l].astype(BF16), w_out[l].astype(BF16), seq=S)

        xf = _ffn(xf, mod3, g_norm[l], w_ffn2_in[l].astype(BF16), w_ffn2_out[l].astype(BF16),
                  g_final if last else None, sub=2, seq=S)
    if depth == 0:
        raise ValueError("depth must be positive")
    return xf.reshape(B, S, D)
```

```python
import functools
import math

import jax
import jax.numpy as jnp
from jax import lax
from jax.experimental import pallas as pl
from jax.experimental.pallas import tpu as pltpu

F32 = jnp.float32
BF16 = jnp.bfloat16
EPS = 1e-6
NEG = -1e30
LOG2E = math.log2(math.e)

RET_HEADS = 4
RET_CHUNK = 256
DIFF_HEAD_DIM = 64
N_SUB = 3
N_MOD = 3

VMEM_LIMIT_BYTES = 56 * 1024 * 1024
BF16_SUBLANES = 16
LANES = 128


def _resident(shape):
    nd = len(shape)
    return pl.BlockSpec(shape, lambda *_: (0,) * nd, pipeline_mode=pl.Buffered(1))


def _params(*sem):
    return pltpu.CompilerParams(dimension_semantics=sem, vmem_limit_bytes=VMEM_LIMIT_BYTES)


def _silu(a):
    return a / (1.0 + jnp.exp(-a))


def _sigmoid(a):
    return 1.0 / (1.0 + jnp.exp(-a))


def _modulated(x, mod_ref, g_ref, sub):
    shift = mod_ref[0, N_MOD * sub:N_MOD * sub + 1, :]
    scale = mod_ref[0, N_MOD * sub + 1:N_MOD * sub + 2, :]
    ms = jnp.mean(x * x, axis=-1, keepdims=True)
    h = x * lax.rsqrt(ms + EPS) * g_ref[sub:sub + 1, :]
    return (h * (1.0 + scale) + shift).astype(BF16)


def _rider(w, nblocks, block_of_step):
    rows = w.shape[0] // nblocks
    assert rows * nblocks == w.shape[0] and rows % BF16_SUBLANES == 0
    spec = pl.BlockSpec((rows, w.shape[1]), lambda *g: (block_of_step(*g), 0))
    return spec, spec, jax.ShapeDtypeStruct(w.shape, BF16)


def _run_riders(in_refs, out_refs):
    for src, dst in zip(in_refs, out_refs):
        dst[...] = src[...].astype(BF16)


def _mod_kernel(c_ref, w_ref, b_ref, o_ref):
    c_act = _silu(c_ref[...]).astype(BF16)
    o_ref[...] = jnp.dot(c_act, w_ref[...].astype(BF16), preferred_element_type=F32) + b_ref[...]


def _mod(c, w, b):
    B, D = c.shape
    N = w.shape[1]
    tn = 1024
    return pl.pallas_call(
        _mod_kernel,
        out_shape=jax.ShapeDtypeStruct((B, N), F32),
        grid=(N // tn,),
        in_specs=[pl.BlockSpec((B, D), lambda j: (0, 0)),
                  pl.BlockSpec((D, tn), lambda j: (0, j)),
                  pl.BlockSpec((1, tn), lambda j: (0, j))],
        out_specs=pl.BlockSpec((B, tn), lambda j: (0, j)),
        compiler_params=_params("arbitrary"),
        name="mod",
    )(c, w, b.reshape(1, N))


def _ffn_kernel(x_ref, mod_ref, g_ref, win_ref, wout_ref, *rest, sub, d_ff, tf, final, n_riders):
    rest = list(rest)
    gfin_ref = rest.pop(0) if final else None
    rider_in = [rest.pop(0) for _ in range(n_riders)]
    o_ref = rest.pop(0)
    rider_out = [rest.pop(0) for _ in range(n_riders)]
    (act_ref,) = rest
    _run_riders(rider_in, rider_out)
    x = x_ref[...]
    h = _modulated(x, mod_ref, g_ref, sub)
    for c in range(d_ff // tf):
        a = jnp.dot(h, win_ref[:, c * tf:(c + 1) * tf], preferred_element_type=F32)
        b = jnp.dot(h, win_ref[:, d_ff + c * tf:d_ff + (c + 1) * tf], preferred_element_type=F32)
        act_ref[:, c * tf:(c + 1) * tf] = (_silu(a) * b).astype(BF16)
    y = jnp.dot(act_ref[...], wout_ref[...], preferred_element_type=F32)
    gate = mod_ref[0, N_MOD * sub + 2:N_MOD * sub + 3, :]
    out = x + (0.5 * gate) * y
    if final:
        ms = jnp.mean(out * out, axis=-1, keepdims=True)
        out = out * lax.rsqrt(ms + EPS) * gfin_ref[...]
    o_ref[...] = out


def _ffn(x2d, mod3, g_norm, w_in, w_out, g_final, *, sub, seq, to_cast=(), tm=512, tf=256):
    M, D = x2d.shape
    d_ff = w_out.shape[0]
    tiles_per_seq = seq // tm
    final = g_final is not None
    in_specs = [pl.BlockSpec((tm, D), lambda i: (i, 0)),
                pl.BlockSpec((1, N_SUB * N_MOD, D), lambda i: (i // tiles_per_seq, 0, 0)),
                _resident(g_norm.shape),
                _resident(w_in.shape),
                _resident(w_out.shape)]
    args = [x2d, mod3, g_norm, w_in, w_out]
    if final:
        in_specs.append(_resident((1, D)))
        args.append(g_final.reshape(1, D))
    riders = [_rider(w, M // tm, lambda i: i) for w in to_cast]
    return pl.pallas_call(
        functools.partial(_ffn_kernel, sub=sub, d_ff=d_ff, tf=tf, final=final, n_riders=len(riders)),
        out_shape=[jax.ShapeDtypeStruct((M, D), F32)] + [r[2] for r in riders],
        grid=(M // tm,),
        in_specs=in_specs + [r[0] for r in riders],
        out_specs=[pl.BlockSpec((tm, D), lambda i: (i, 0))] + [r[1] for r in riders],
        scratch_shapes=[pltpu.VMEM((tm, d_ff), BF16)],
        compiler_params=_params("arbitrary"),
        name="ffn%d" % sub,
    )(*args, *to_cast)


def _inproj_segments(D):
    rqk, rv = D, 2 * D
    return (("rq", rqk, None), ("rk", rqk, "ret_k_scale"), ("rv", rv, None), ("rg", rv, "silu"),
            ("dq", D, "diff_q_scale"), ("dk", D, None), ("dv", D, None), ("gates", 2 * D, "sigmoid"))


def _inproj_kernel(x_ref, mod_ref, g_ref, w_ref, *out_refs, segs, tn, ret_k_scale, diff_q_scale):
    h = _modulated(x_ref[...], mod_ref, g_ref, 1)
    col = 0
    for (_, width, epi), o_ref in zip(segs, out_refs):
        for c in range(width // tn):
            y = jnp.dot(h, w_ref[:, col + c * tn:col + (c + 1) * tn], preferred_element_type=F32)
            if epi == "silu":
                y = _silu(y)
            elif epi == "sigmoid":
                y = _sigmoid(y)
            elif epi == "ret_k_scale":
                y = y * ret_k_scale
            elif epi == "diff_q_scale":
                y = y * diff_q_scale
            o_ref[:, c * tn:(c + 1) * tn] = y.astype(BF16)
        col += width


def _inproj(x2d, mod3, g_norm, w_in, *, seq, tm=256, tn=512):
    M, D = x2d.shape
    segs = _inproj_segments(D)
    assert sum(s[1] for s in segs) == w_in.shape[1]
    tiles_per_seq = seq // tm
    kern = functools.partial(
        _inproj_kernel, segs=segs, tn=tn,
        ret_k_scale=float((D // RET_HEADS) ** -0.5),
        diff_q_scale=LOG2E * DIFF_HEAD_DIM ** -0.5)
    return pl.pallas_call(
        kern,
        out_shape=[jax.ShapeDtypeStruct((M, s[1]), BF16) for s in segs],
        grid=(M // tm,),
        in_specs=[pl.BlockSpec((tm, D), lambda i: (i, 0)),
                  pl.BlockSpec((1, N_SUB * N_MOD, D), lambda i: (i // tiles_per_seq, 0, 0)),
                  _resident(g_norm.shape),
                  _resident(w_in.shape)],
        out_specs=[pl.BlockSpec((tm, s[1]), lambda i: (i, 0)) for s in segs],
        compiler_params=_params("arbitrary"),
        name="inproj",
    )(x2d, mod3, g_norm, w_in)


def _ret_kernel(q_ref, k_ref, v_ref, sg_ref, dec_ref, xi_ref, zeta_ref, gch_ref, *rest,
                heads, dk, dv, n_riders):
    rider_in, (o_ref, *rider_out), state_ref = rest[:n_riders], rest[n_riders:-1], rest[-1]
    _run_riders(rider_in, rider_out)

    @pl.when(pl.program_id(1) == 0)
    def _():
        state_ref[...] = jnp.zeros_like(state_ref)

    C = q_ref.shape[1]
    hs = range(heads)
    q = [q_ref[0, :, h * dk:(h + 1) * dk] for h in hs]
    k = [k_ref[0, :, h * dk:(h + 1) * dk] for h in hs]
    v = [v_ref[0, :, h * dv:(h + 1) * dv] for h in hs]
    def first_matmuls(h):
        s = lax.dot_general(q[h], k[h], (((1,), (1,)), ((), ())), preferred_element_type=F32)
        cross = jnp.dot(q[h], state_ref[h].astype(BF16), preferred_element_type=F32)
        return s, cross

    def stacked_lhs(h, s):
        intra = (s * dec_ref[h]).astype(BF16)
        kz_t = (k[h].astype(F32) * zeta_ref[h]).T.astype(BF16)
        return jnp.concatenate([intra, kz_t], axis=0)

    def finish(h, both, cross):
        state_ref[h] = state_ref[h] * gch_ref[h] + both[C:, :]
        o = both[:C, :] + cross * xi_ref[h]
        mu = jnp.mean(o, axis=-1, keepdims=True)
        d = o - mu
        var = jnp.mean(d * d, axis=-1, keepdims=True)
        on = d * lax.rsqrt(var + EPS)
        o_ref[0, :, h * dv:(h + 1) * dv] = (on * sg_ref[0, :, h * dv:(h + 1) * dv].astype(F32)).astype(BF16)

    for h0 in range(0, heads, 2):
        pair = range(h0, min(h0 + 2, heads))
        first = [first_matmuls(h) for h in pair]
        both = [jnp.dot(stacked_lhs(h, s), v[h], preferred_element_type=F32)
                for h, (s, _) in zip(pair, first)]
        for h, b, (_, cross) in zip(pair, both, first):
            finish(h, b, cross)


def _retention_tables(heads, C):
    hh = jnp.arange(heads, dtype=F32)
    log_g = jnp.log1p(-jnp.exp2(-5.0 - hh))
    pos = jnp.arange(C, dtype=F32)
    rel = pos[:, None] - pos[None, :]
    decay = jnp.where(rel[None] >= 0, jnp.exp(jnp.maximum(rel, 0.0)[None] * log_g[:, None, None]), 0.0)
    xi = jnp.exp((pos + 1.0)[None, :] * log_g[:, None])[..., None]
    zeta = jnp.exp((C - 1.0 - pos)[None, :] * log_g[:, None])[..., None]
    g_chunk = jnp.exp(C * log_g)[:, None, None]
    return decay, xi, zeta, g_chunk


def _retention(rq, rk, rv, sg, *, heads, C, to_cast=()):
    B, S, qk_w = rq.shape
    v_w = rv.shape[-1]
    dk, dv = qk_w // heads, v_w // heads
    nchunks = S // C
    decay, xi, zeta, g_chunk = _retention_tables(heads, C)
    qk_spec = pl.BlockSpec((1, C, qk_w), lambda b, n: (b, n, 0))
    v_spec = pl.BlockSpec((1, C, v_w), lambda b, n: (b, n, 0))
    riders = [_rider(w, B * nchunks, lambda b, n: b * nchunks + n) for w in to_cast]
    return pl.pallas_call(
        functools.partial(_ret_kernel, heads=heads, dk=dk, dv=dv, n_riders=len(riders)),
        out_shape=[jax.ShapeDtypeStruct((B, S, v_w), BF16)] + [r[2] for r in riders],
        grid=(B, nchunks),
        in_specs=[qk_spec, qk_spec, v_spec, v_spec,
                  _resident(decay.shape), _resident(xi.shape), _resident(zeta.shape),
                  _resident(g_chunk.shape)] + [r[0] for r in riders],
        out_specs=[v_spec] + [r[1] for r in riders],
        scratch_shapes=[pltpu.VMEM((heads, dk, dv), F32)],
        compiler_params=_params("arbitrary", "arbitrary"),
        name="retention",
    )(rq, rk, rv, sg, decay, xi, zeta, g_chunk, *to_cast)


STEPS_PER_ITER = 4


def _diff_kernel(q_ref, k_ref, v_ref, lam_ref, subln_ref, slope_ref, *rest,
                 T, seq, dh, lam_init, n_step_riders, n_row_riders):
    n_riders = n_step_riders + n_row_riders
    rider_in, o_ref, rider_out = rest[:n_riders], rest[n_riders], rest[n_riders + 1:2 * n_riders + 1]
    vt_ref, wq_ref, kaug_ref, acc_ref, m_ref, alpha_ref, *buf_refs = rest[2 * n_riders + 1:]
    _run_riders(rider_in[:n_step_riders], rider_out[:n_step_riders])

    @pl.when(pl.program_id(1) == 0)
    def _():
        _run_riders(rider_in[n_step_riders:], rider_out[n_step_riders:])

    nblk = seq // T
    n_off = nblk * (nblk - 1) // 2
    dv = 2 * dh
    dv_ext = dv + BF16_SUBLANES
    dummy = nblk
    lp = lam_ref[...]
    lam = (jnp.exp(jnp.sum(lp[0:1] * lp[1:2], axis=-1, keepdims=True))
           - jnp.exp(jnp.sum(lp[2:3] * lp[3:4], axis=-1, keepdims=True)) + lam_init)

    row = lax.broadcasted_iota(jnp.int32, (T, LANES), 0)
    lane = lax.broadcasted_iota(jnp.int32, (T, LANES), 1)
    kaug_ref[...] = jnp.where(lane < 3, row, 0).astype(F32).astype(BF16)

    slope = slope_ref[0]
    s_hi = slope.astype(BF16).astype(F32)
    s_mid = (slope - s_hi).astype(BF16).astype(F32)
    s_lo = slope - s_hi - s_mid
    aug_row = lax.broadcasted_iota(jnp.int32, (LANES, 2 * T), 0)
    slope_rows = jnp.where(aug_row == 0, s_hi, jnp.where(aug_row == 1, s_mid,
                           jnp.where(aug_row == 2, s_lo, 0.0))).astype(BF16)
    ones_rows = jnp.where(lax.broadcasted_iota(jnp.int32, (BF16_SUBLANES, T), 0) == 0, 1.0, 0.0).astype(BF16)
    feat_id = lax.broadcasted_iota(jnp.int32, (dv, T), 0)
    for j in range(nblk):
        rows = slice(j * T, (j + 1) * T)
        vt_ref[j, 0:dv, :] = v_ref[0, rows, :].T
        vt_ref[j, dv:dv_ext, :] = ones_rows
        qt = q_ref[0, rows, :].T
        zero = jnp.zeros_like(qt)
        wq_ref[j, 0:dv, :] = jnp.concatenate([jnp.where(feat_id < dh, qt, zero),
                                              jnp.where(feat_id >= dh, qt, zero)], axis=1)
        wq_ref[j, dv:dv + LANES, :] = slope_rows

    def scores(qi, j):
        k0 = j * T if isinstance(j, int) else pl.multiple_of(j * T, T)
        k_blk = k_ref[0, pl.ds(k0, T), :]
        return jnp.dot(jnp.concatenate([k_blk, kaug_ref[...]], axis=1), wq_ref[qi],
                       preferred_element_type=F32)

    key_id = lax.broadcasted_iota(jnp.int32, (T, 2 * T), 0)
    col_id = lax.broadcasted_iota(jnp.int32, (T, 2 * T), 1)
    causal = key_id <= jnp.where(col_id >= T, col_id - T, col_id)

    acc_ref[nblk - 1] = jnp.zeros((dv_ext, 2 * T), F32)

    U = STEPS_PER_ITER
    bufs = [buf_refs[3 * u:3 * u + 3] for u in range(U)]
    bufs[U - 1][2][...] = jnp.zeros((T, 2 * T), BF16)

    def store_scores(t_ref, tmax_ref, t, masked):
        if masked:
            t = jnp.where(causal, t, NEG)
        t_ref[...] = t
        tmax_ref[...] = jnp.max(t, axis=0, keepdims=True)

    store_scores(bufs[0][0], bufs[0][1], scores(0, 0), True)

    def next_block(qi, j):
        on_diag = j == qi
        last_diag = qi == nblk - 1
        end_of_tile = j == qi - 1
        qi_next = jnp.where(on_diag, jnp.where(last_diag, 1, qi + 1), jnp.where(end_of_tile, qi + 1, qi))
        j_next = jnp.where(on_diag, jnp.where(last_diag, 0, qi + 1), jnp.where(end_of_tile, 0, j + 1))
        return qi_next, j_next

    def step(carry, u, masked, next_masked):
        t_cur_ref, tmax_cur_ref, p_cur_ref = bufs[u]
        t_nxt_ref, tmax_nxt_ref, _ = bufs[(u + 1) % U]
        p_prv_ref = bufs[(u - 1) % U][2]
        qi, j, qi_prev, j_prev = carry
        qi_next, j_next = next_block(qi, j)

        store_scores(t_nxt_ref, tmax_nxt_ref, scores(jnp.minimum(qi_next, nblk - 1), j_next), next_masked)

        pv = jnp.dot(vt_ref[j_prev], p_prv_ref[...], preferred_element_type=F32)
        if masked:
            acc_ref[qi_prev] = pv
        else:
            acc_ref[qi_prev] = alpha_ref[...] * acc_ref[qi_prev] + pv

        t = t_cur_ref[...]
        if masked:
            m_new = tmax_cur_ref[...]
            alpha_ref[...] = jnp.zeros_like(alpha_ref)
            p_cur_ref[...] = jnp.exp2(t - m_new).astype(BF16)
        else:
            m_old = m_ref[qi]
            shift = slope * ((j - qi) * T).astype(F32)
            m_new = jnp.maximum(m_old, tmax_cur_ref[...] + shift)
            alpha_ref[...] = jnp.exp2(m_old - m_new)
            p_cur_ref[...] = jnp.exp2(t + (shift - m_new)).astype(BF16)
        m_ref[qi] = m_new
        return qi_next, j_next, qi, j

    def steps(masked, last_next_masked, _, carry):
        for u in range(U):
            carry = step(carry, u, masked, masked if u + 1 < U else last_next_masked)
        return carry

    carry = (jnp.int32(0), jnp.int32(0), jnp.int32(dummy), jnp.int32(0))
    carry = lax.fori_loop(0, nblk // STEPS_PER_ITER - 1, functools.partial(steps, True, True), carry)
    carry = steps(True, False, 0, carry)
    carry = lax.fori_loop(0, n_off // STEPS_PER_ITER, functools.partial(steps, False, False), carry)
    _, _, qi_last, j_last = carry
    acc_ref[qi_last] = (alpha_ref[...] * acc_ref[qi_last]
                        + jnp.dot(vt_ref[j_last], bufs[U - 1][2][...], preferred_element_type=F32))

    gain = jnp.broadcast_to(subln_ref[...] * (1.0 - lam_init), (dv, T))
    for qi in range(nblk):
        acc = acc_ref[qi]
        r = 1.0 / acc[dv:dv + 1, :]
        out = acc[0:dv, :T] * r[:, :T] - acc[0:dv, T:] * (lam * r[:, T:])
        ms = jnp.mean(out * out, axis=0, keepdims=True)
        o_ref[0, qi * T:(qi + 1) * T, :] = (out * lax.rsqrt(ms + EPS) * gain).T.astype(BF16)


def _diff_attention(dq, dk, dv, diff_lambda, subln, *, heads, lam_init, cast_by_step=(), cast_by_row=(),
                    T=256):
    B, S, W = dq.shape
    hw = W // heads
    dh = hw // 2
    nblk = S // T
    assert T <= 256 and hw == LANES
    assert STEPS_PER_ITER % 2 == 0 and nblk % STEPS_PER_ITER == 0
    assert (nblk * (nblk - 1) // 2) % STEPS_PER_ITER == 0
    slopes = jnp.exp2(-8.0 * (jnp.arange(heads, dtype=F32) + 1.0) / heads) * LOG2E
    head_spec = pl.BlockSpec((1, S, hw), lambda b, h: (b, 0, h))
    riders = ([_rider(w, B * heads, lambda b, h: b * heads + h) for w in cast_by_step]
              + [_rider(w, B, lambda b, h: b) for w in cast_by_row])
    return pl.pallas_call(
        functools.partial(_diff_kernel, T=T, seq=S, dh=dh, lam_init=lam_init,
                          n_step_riders=len(cast_by_step), n_row_riders=len(cast_by_row)),
        out_shape=[jax.ShapeDtypeStruct((B, S, W), BF16)] + [r[2] for r in riders],
        grid=(B, heads),
        in_specs=[head_spec, head_spec, head_spec,
                  _resident(diff_lambda.shape),
                  _resident((hw, 1)),
                  pl.BlockSpec((1, 1, 1), lambda b, h: (h, 0, 0))] + [r[0] for r in riders],
        out_specs=[head_spec] + [r[1] for r in riders],
        scratch_shapes=[pltpu.VMEM((nblk, hw + BF16_SUBLANES, T), BF16),
                        pltpu.VMEM((nblk, hw + LANES, 2 * T), BF16),
                        pltpu.VMEM((T, LANES), BF16),
                        pltpu.VMEM((nblk + 1, hw + BF16_SUBLANES, 2 * T), F32),
                        pltpu.VMEM((nblk, 1, 2 * T), F32),
                        pltpu.VMEM((1, 2 * T), F32)]
                       + [pltpu.VMEM((T, 2 * T), F32),
                          pltpu.VMEM((1, 2 * T), F32),
                          pltpu.VMEM((T, 2 * T), BF16)] * STEPS_PER_ITER,
        compiler_params=_params("arbitrary", "arbitrary"),
        name="diffattn",
    )(dq, dk, dv, diff_lambda, subln.reshape(hw, 1), slopes.reshape(heads, 1, 1),
      *cast_by_step, *cast_by_row)


def _mixout_kernel(x_ref, mod_ref, ret_ref, dif_ref, gs_ref, wro_ref, wdo_ref, wo_ref, o_ref):
    D = x_ref.shape[-1]
    y_ret = jnp.dot(ret_ref[...], wro_ref[...], preferred_element_type=F32)
    y_dif = jnp.dot(dif_ref[...], wdo_ref[...], preferred_element_type=F32)
    y = gs_ref[:, :D].astype(F32) * y_ret + gs_ref[:, D:].astype(F32) * y_dif
    gate = mod_ref[0, N_MOD * 1 + 2:N_MOD * 1 + 3, :]
    o_ref[...] = x_ref[...] + gate * jnp.dot(y.astype(BF16), wo_ref[...], preferred_element_type=F32)


def _mixout(x2d, mod3, ret2d, dif2d, gs2d, w_ro, w_do, w_o, *, seq, tm=512):
    M, D = x2d.shape
    tiles_per_seq = seq // tm
    row = lambda w: pl.BlockSpec((tm, w), lambda i: (i, 0))
    return pl.pallas_call(
        _mixout_kernel,
        out_shape=jax.ShapeDtypeStruct((M, D), F32),
        grid=(M // tm,),
        in_specs=[row(D),
                  pl.BlockSpec((1, N_SUB * N_MOD, D), lambda i: (i // tiles_per_seq, 0, 0)),
                  row(ret2d.shape[1]), row(dif2d.shape[1]), row(gs2d.shape[1]),
                  _resident(w_ro.shape), _resident(w_do.shape), _resident(w_o.shape)],
        out_specs=row(D),
        compiler_params=_params("arbitrary"),
        name="mixout",
    )(x2d, mod3, ret2d, dif2d, gs2d, w_ro, w_do, w_o)


def kernel(x, c, w_cond, b_cond, g_norm, w_ffn1_in, w_ffn1_out, w_in, w_ret_out, diff_lambda,
           diff_subln, w_diff_out, w_out, w_ffn2_in, w_ffn2_out, g_final):
    B, S, D = x.shape
    depth = w_cond.shape[0]
    diff_heads = D // (2 * DIFF_HEAD_DIM)
    xf = x.reshape(B * S, D)
    for l in range(depth):
        lam_init = 0.8 - 0.6 * math.exp(-0.3 * l)
        mod3 = _mod(c, w_cond[l], b_cond[l]).reshape(B, N_SUB * N_MOD, D)
        last = l == depth - 1

        xf, w_in_bf = _ffn(xf, mod3, g_norm[l], w_ffn1_in[l].astype(BF16), w_ffn1_out[l].astype(BF16),
                           None, sub=0, seq=S, to_cast=(w_in[l],))

        rq, rk, rv, sg, dq, dk, dv, gs = _inproj(xf, mod3, g_norm[l], w_in_bf, seq=S)
        b3 = lambda t: t.reshape(B, S, t.shape[-1])
        ret, w_ro_bf, w_do_bf, w_o_bf = _retention(
            b3(rq), b3(rk), b3(rv), b3(sg), heads=RET_HEADS, C=RET_CHUNK,
            to_cast=(w_ret_out[l], w_diff_out[l], w_out[l]))
        dif, w_f2i_bf, w_f2o_bf = _diff_attention(
            b3(dq), b3(dk), b3(dv), diff_lambda[l], diff_subln[l], heads=diff_heads, lam_init=lam_init,
            cast_by_step=(w_ffn2_in[l],), cast_by_row=(w_ffn2_out[l],))
        xf = _mixout(xf, mod3, ret.reshape(B * S, -1), dif.reshape(B * S, -1), gs,
                     w_ro_bf, w_do_bf, w_o_bf, seq=S)

        (xf,) = _ffn(xf, mod3, g_norm[l], w_f2i_bf, w_f2o_bf, g_final if last else None, sub=2, seq=S)
    if depth == 0:
        raise ValueError("depth must be positive")
    return xf.reshape(B, S, D)
```

```python
import functools
import math

import jax
import jax.numpy as jnp
from jax import lax
from jax.experimental import pallas as pl
from jax.experimental.pallas import tpu as pltpu

F32 = jnp.float32
BF16 = jnp.bfloat16
EPS = 1e-6
NEG = -1e30
LOG2E = math.log2(math.e)

RET_HEADS = 4
RET_CHUNK = 256
DIFF_HEAD_DIM = 64
N_SUB = 3
N_MOD = 3

VMEM_LIMIT_BYTES = 56 * 1024 * 1024
BF16_SUBLANES = 16
LANES = 128


def _resident(shape):
    nd = len(shape)
    return pl.BlockSpec(shape, lambda *_: (0,) * nd, pipeline_mode=pl.Buffered(1))


def _params(*sem):
    return pltpu.CompilerParams(dimension_semantics=sem, vmem_limit_bytes=VMEM_LIMIT_BYTES)


def _silu(a):
    return a / (1.0 + jnp.exp(-a))


def _sigmoid(a):
    return 1.0 / (1.0 + jnp.exp(-a))


def _modulated(x, mod_ref, g_ref, sub):
    shift = mod_ref[0, N_MOD * sub:N_MOD * sub + 1, :]
    scale = mod_ref[0, N_MOD * sub + 1:N_MOD * sub + 2, :]
    ms = jnp.mean(x * x, axis=-1, keepdims=True)
    h = x * lax.rsqrt(ms + EPS) * g_ref[sub:sub + 1, :]
    return (h * (1.0 + scale) + shift).astype(BF16)


def _rider(w, nblocks, block_of_step):
    rows = w.shape[0] // nblocks
    assert rows * nblocks == w.shape[0] and rows % BF16_SUBLANES == 0
    spec = pl.BlockSpec((rows, w.shape[1]), lambda *g: (block_of_step(*g), 0))
    return spec, spec, jax.ShapeDtypeStruct(w.shape, BF16)


def _run_riders(in_refs, out_refs):
    for src, dst in zip(in_refs, out_refs):
        dst[...] = src[...].astype(BF16)


def _mod_kernel(c_ref, w_ref, b_ref, o_ref):
    c_act = _silu(c_ref[...]).astype(BF16)
    o_ref[...] = jnp.dot(c_act, w_ref[...].astype(BF16), preferred_element_type=F32) + b_ref[...]


def _mod(c, w, b):
    B, D = c.shape
    N = w.shape[1]
    tn = 1024
    return pl.pallas_call(
        _mod_kernel,
        out_shape=jax.ShapeDtypeStruct((B, N), F32),
        grid=(N // tn,),
        in_specs=[pl.BlockSpec((B, D), lambda j: (0, 0)),
                  pl.BlockSpec((D, tn), lambda j: (0, j)),
                  pl.BlockSpec((1, tn), lambda j: (0, j))],
        out_specs=pl.BlockSpec((B, tn), lambda j: (0, j)),
        compiler_params=_params("arbitrary"),
        name="mod",
    )(c, w, b.reshape(1, N))


def _ffn_kernel(x_ref, mod_ref, g_ref, win_ref, wout_ref, *rest, sub, d_ff, tf, final, n_riders):
    rest = list(rest)
    gfin_ref = rest.pop(0) if final else None
    rider_in = [rest.pop(0) for _ in range(n_riders)]
    o_ref = rest.pop(0)
    rider_out = [rest.pop(0) for _ in range(n_riders)]
    (act_ref,) = rest
    _run_riders(rider_in, rider_out)
    x = x_ref[...]
    h = _modulated(x, mod_ref, g_ref, sub)
    for c in range(d_ff // tf):
        a = jnp.dot(h, win_ref[:, c * tf:(c + 1) * tf], preferred_element_type=F32)
        b = jnp.dot(h, win_ref[:, d_ff + c * tf:d_ff + (c + 1) * tf], preferred_element_type=F32)
        act_ref[:, c * tf:(c + 1) * tf] = (_silu(a) * b).astype(BF16)
    y = jnp.dot(act_ref[...], wout_ref[...], preferred_element_type=F32)
    gate = mod_ref[0, N_MOD * sub + 2:N_MOD * sub + 3, :]
    out = x + (0.5 * gate) * y
    if final:
        ms = jnp.mean(out * out, axis=-1, keepdims=True)
        out = out * lax.rsqrt(ms + EPS) * gfin_ref[...]
    o_ref[...] = out


def _ffn(x2d, mod3, g_norm, w_in, w_out, g_final, *, sub, seq, to_cast=(), tm=512, tf=256):
    M, D = x2d.shape
    d_ff = w_out.shape[0]
    tiles_per_seq = seq // tm
    final = g_final is not None
    in_specs = [pl.BlockSpec((tm, D), lambda i: (i, 0)),
                pl.BlockSpec((1, N_SUB * N_MOD, D), lambda i: (i // tiles_per_seq, 0, 0)),
                _resident(g_norm.shape),
                _resident(w_in.shape),
                _resident(w_out.shape)]
    args = [x2d, mod3, g_norm, w_in, w_out]
    if final:
        in_specs.append(_resident((1, D)))
        args.append(g_final.reshape(1, D))
    riders = [_rider(w, M // tm, lambda i: i) for w in to_cast]
    return pl.pallas_call(
        functools.partial(_ffn_kernel, sub=sub, d_ff=d_ff, tf=tf, final=final, n_riders=len(riders)),
        out_shape=[jax.ShapeDtypeStruct((M, D), F32)] + [r[2] for r in riders],
        grid=(M // tm,),
        in_specs=in_specs + [r[0] for r in riders],
        out_specs=[pl.BlockSpec((tm, D), lambda i: (i, 0))] + [r[1] for r in riders],
        scratch_shapes=[pltpu.VMEM((tm, d_ff), BF16)],
        compiler_params=_params("arbitrary"),
        name="ffn%d" % sub,
    )(*args, *to_cast)


def _inproj_segments(D):
    rqk, rv = D, 2 * D
    return (("rq", rqk, None, False), ("rk", rqk, "ret_k_scale", True), ("rv", rv, None, False),
            ("rg", rv, "silu", False), ("dq", D, "diff_q_scale", True), ("dk", D, None, False),
            ("dv", D, None, True), ("gates", 2 * D, "sigmoid", False))


def _inproj_kernel(x_ref, mod_ref, g_ref, w_ref, *out_refs, segs, tn, ret_k_scale, diff_q_scale):
    h = _modulated(x_ref[...], mod_ref, g_ref, 1)
    col = 0
    for (_, width, epi, transposed), o_ref in zip(segs, out_refs):
        for c in range(width // tn):
            y = jnp.dot(h, w_ref[:, col + c * tn:col + (c + 1) * tn], preferred_element_type=F32)
            if epi == "silu":
                y = _silu(y)
            elif epi == "sigmoid":
                y = _sigmoid(y)
            elif epi == "ret_k_scale":
                y = y * ret_k_scale
            elif epi == "diff_q_scale":
                y = y * diff_q_scale
            if transposed:
                o_ref[0, c * tn:(c + 1) * tn, :] = y.astype(BF16).T
            else:
                o_ref[:, c * tn:(c + 1) * tn] = y.astype(BF16)
        col += width


def _inproj(x2d, mod3, g_norm, w_in, *, batch, seq, tm=256, tn=512):
    M, D = x2d.shape
    segs = _inproj_segments(D)
    assert sum(s[1] for s in segs) == w_in.shape[1]
    tiles_per_seq = seq // tm
    kern = functools.partial(
        _inproj_kernel, segs=segs, tn=tn,
        ret_k_scale=float((D // RET_HEADS) ** -0.5),
        diff_q_scale=LOG2E * DIFF_HEAD_DIM ** -0.5)
    out_shape, out_specs = [], []
    for _, width, _, transposed in segs:
        if transposed:
            out_shape.append(jax.ShapeDtypeStruct((batch, width, seq), BF16))
            out_specs.append(pl.BlockSpec((1, width, tm),
                                          lambda i: (i // tiles_per_seq, 0, i % tiles_per_seq)))
        else:
            out_shape.append(jax.ShapeDtypeStruct((M, width), BF16))
            out_specs.append(pl.BlockSpec((tm, width), lambda i: (i, 0)))
    return pl.pallas_call(
        kern,
        out_shape=out_shape,
        grid=(M // tm,),
        in_specs=[pl.BlockSpec((tm, D), lambda i: (i, 0)),
                  pl.BlockSpec((1, N_SUB * N_MOD, D), lambda i: (i // tiles_per_seq, 0, 0)),
                  _resident(g_norm.shape),
                  _resident(w_in.shape)],
        out_specs=out_specs,
        compiler_params=_params("arbitrary"),
        name="inproj",
    )(x2d, mod3, g_norm, w_in)


def _ret_kernel(q_ref, kt_ref, v_ref, sg_ref, dec_ref, xi_ref, zeta_ref, gch_ref, *rest,
                heads, dk, dv, n_riders):
    rider_in, (o_ref, *rider_out), state_ref = rest[:n_riders], rest[n_riders:-1], rest[-1]
    _run_riders(rider_in, rider_out)

    @pl.when(pl.program_id(1) == 0)
    def _():
        state_ref[...] = jnp.zeros_like(state_ref)

    C = q_ref.shape[1]
    hs = range(heads)
    q = [q_ref[0, :, h * dk:(h + 1) * dk] for h in hs]
    kt = [kt_ref[0, h * dk:(h + 1) * dk, :] for h in hs]
    v = [v_ref[0, :, h * dv:(h + 1) * dv] for h in hs]

    def first_matmuls(h):
        s = jnp.dot(q[h], kt[h], preferred_element_type=F32)
        cross = jnp.dot(q[h], state_ref[h].astype(BF16), preferred_element_type=F32)
        return s, cross

    def stacked_lhs(h, s):
        intra = (s * dec_ref[h]).astype(BF16)
        kz_t = (kt[h].astype(F32) * zeta_ref[h]).astype(BF16)
        return jnp.concatenate([intra, kz_t], axis=0)

    def finish(h, both, cross):
        state_ref[h] = state_ref[h] * gch_ref[h] + both[C:, :]
        xi = jnp.concatenate([xi_ref[h]] * (dv // LANES), axis=1)
        o = both[:C, :] + cross * xi
        mu = jnp.mean(o, axis=-1, keepdims=True)
        d = o - mu
        var = jnp.mean(d * d, axis=-1, keepdims=True)
        on = d * lax.rsqrt(var + EPS)
        o_ref[0, :, h * dv:(h + 1) * dv] = (on * sg_ref[0, :, h * dv:(h + 1) * dv].astype(F32)).astype(BF16)

    for h0 in range(0, heads, 2):
        pair = range(h0, min(h0 + 2, heads))
        first = [first_matmuls(h) for h in pair]
        both = [jnp.dot(stacked_lhs(h, s), v[h], preferred_element_type=F32)
                for h, (s, _) in zip(pair, first)]
        for h, b, (_, cross) in zip(pair, both, first):
            finish(h, b, cross)


def _retention_tables(heads, C):
    hh = jnp.arange(heads, dtype=F32)
    log_g = jnp.log1p(-jnp.exp2(-5.0 - hh))
    pos = jnp.arange(C, dtype=F32)
    rel = pos[:, None] - pos[None, :]
    decay = jnp.where(rel[None] >= 0, jnp.exp(jnp.maximum(rel, 0.0)[None] * log_g[:, None, None]), 0.0)
    xi = jnp.exp((pos + 1.0)[None, :] * log_g[:, None])[..., None]
    zeta = jnp.exp((C - 1.0 - pos)[None, :] * log_g[:, None])[:, None, :]
    g_chunk = jnp.exp(C * log_g)[:, None, None]
    return decay, jnp.broadcast_to(xi, (heads, C, LANES)), zeta, g_chunk


def _retention(rq, rkt, rv, sg, *, heads, C, to_cast=()):
    B, S, qk_w = rq.shape
    v_w = rv.shape[-1]
    dk, dv = qk_w // heads, v_w // heads
    nchunks = S // C
    decay, xi, zeta, g_chunk = _retention_tables(heads, C)
    q_spec = pl.BlockSpec((1, C, qk_w), lambda b, n: (b, n, 0))
    kt_spec = pl.BlockSpec((1, qk_w, C), lambda b, n: (b, 0, n))
    v_spec = pl.BlockSpec((1, C, v_w), lambda b, n: (b, n, 0))
    riders = [_rider(w, B * nchunks, lambda b, n: b * nchunks + n) for w in to_cast]
    return pl.pallas_call(
        functools.partial(_ret_kernel, heads=heads, dk=dk, dv=dv, n_riders=len(riders)),
        out_shape=[jax.ShapeDtypeStruct((B, S, v_w), BF16)] + [r[2] for r in riders],
        grid=(B, nchunks),
        in_specs=[q_spec, kt_spec, v_spec, v_spec,
                  _resident(decay.shape), _resident(xi.shape), _resident(zeta.shape),
                  _resident(g_chunk.shape)] + [r[0] for r in riders],
        out_specs=[v_spec] + [r[1] for r in riders],
        scratch_shapes=[pltpu.VMEM((heads, dk, dv), F32)],
        compiler_params=_params("arbitrary", "arbitrary"),
        name="retention",
    )(rq, rkt, rv, sg, decay, xi, zeta, g_chunk, *to_cast)


STEPS_PER_ITER = 4


def _diff_kernel(qt_ref, k_ref, vt_in_ref, lam_ref, subln_ref, slope_ref, *rest,
                 T, seq, dh, lam_init, n_step_riders, n_row_riders):
    n_riders = n_step_riders + n_row_riders
    rider_in, o_ref, rider_out = rest[:n_riders], rest[n_riders], rest[n_riders + 1:2 * n_riders + 1]
    vt_ref, wq_ref, kaug_ref, acc_ref, m_ref, alpha_ref, *buf_refs = rest[2 * n_riders + 1:]
    _run_riders(rider_in[:n_step_riders], rider_out[:n_step_riders])

    @pl.when(pl.program_id(1) == 0)
    def _():
        _run_riders(rider_in[n_step_riders:], rider_out[n_step_riders:])

    nblk = seq // T
    n_off = nblk * (nblk - 1) // 2
    dv = 2 * dh
    dv_ext = dv + BF16_SUBLANES
    dummy = nblk
    lp = lam_ref[...]
    lam = (jnp.exp(jnp.sum(lp[0:1] * lp[1:2], axis=-1, keepdims=True))
           - jnp.exp(jnp.sum(lp[2:3] * lp[3:4], axis=-1, keepdims=True)) + lam_init)

    row = lax.broadcasted_iota(jnp.int32, (T, LANES), 0)
    lane = lax.broadcasted_iota(jnp.int32, (T, LANES), 1)
    kaug_ref[...] = jnp.where(lane < 3, row, 0).astype(F32).astype(BF16)
    slope = slope_ref[0]
    s_hi = slope.astype(BF16).astype(F32)
    s_mid = (slope - s_hi).astype(BF16).astype(F32)
    s_lo = slope - s_hi - s_mid
    aug_row = lax.broadcasted_iota(jnp.int32, (LANES, 2 * T), 0)
    slope_rows = jnp.where(aug_row == 0, s_hi, jnp.where(aug_row == 1, s_mid,
                           jnp.where(aug_row == 2, s_lo, 0.0))).astype(BF16)
    ones_rows = jnp.where(lax.broadcasted_iota(jnp.int32, (BF16_SUBLANES, T), 0) == 0, 1.0, 0.0).astype(BF16)
    feat_id = lax.broadcasted_iota(jnp.int32, (dv, T), 0)
    for j in range(nblk):
        cols = slice(j * T, (j + 1) * T)
        vt_ref[j, 0:dv, :] = vt_in_ref[0, :, cols]
        vt_ref[j, dv:dv_ext, :] = ones_rows
        qt = qt_ref[0, :, cols]
        zero = jnp.zeros_like(qt)
        wq_ref[j, 0:dv, :] = jnp.concatenate([jnp.where(feat_id < dh, qt, zero),
                                              jnp.where(feat_id >= dh, qt, zero)], axis=1)
        wq_ref[j, dv:dv + LANES, :] = slope_rows

    def scores(qi, j):
        k0 = j * T if isinstance(j, int) else pl.multiple_of(j * T, T)
        k_blk = k_ref[0, pl.ds(k0, T), :]
        return jnp.dot(jnp.concatenate([k_blk, kaug_ref[...]], axis=1), wq_ref[qi],
                       preferred_element_type=F32)

    key_id = lax.broadcasted_iota(jnp.int32, (T, 2 * T), 0)
    col_id = lax.broadcasted_iota(jnp.int32, (T, 2 * T), 1)
    causal = key_id <= jnp.where(col_id >= T, col_id - T, col_id)

    acc_ref[nblk - 1] = jnp.zeros((dv_ext, 2 * T), F32)

    U = STEPS_PER_ITER
    bufs = [buf_refs[3 * u:3 * u + 3] for u in range(U)]
    bufs[U - 1][2][...] = jnp.zeros((T, 2 * T), BF16)

    def store_scores(t_ref, tmax_ref, t, masked):
        if masked:
            t = jnp.where(causal, t, NEG)
        t_ref[...] = t
        tmax_ref[...] = jnp.max(t, axis=0, keepdims=True)

    store_scores(bufs[0][0], bufs[0][1], scores(0, 0), True)

    def next_block(qi, j):
        on_diag = j == qi
        last_diag = qi == nblk - 1
        end_of_tile = j == qi - 1
        qi_next = jnp.where(on_diag, jnp.where(last_diag, 1, qi + 1), jnp.where(end_of_tile, qi + 1, qi))
        j_next = jnp.where(on_diag, jnp.where(last_diag, 0, qi + 1), jnp.where(end_of_tile, 0, j + 1))
        return qi_next, j_next

    def step(carry, u, masked, next_masked):
        t_cur_ref, tmax_cur_ref, p_cur_ref = bufs[u]
        t_nxt_ref, tmax_nxt_ref, _ = bufs[(u + 1) % U]
        p_prv_ref = bufs[(u - 1) % U][2]
        qi, j, qi_prev, j_prev = carry
        qi_next, j_next = next_block(qi, j)

        store_scores(t_nxt_ref, tmax_nxt_ref, scores(jnp.minimum(qi_next, nblk - 1), j_next), next_masked)

        pv = jnp.dot(vt_ref[j_prev], p_prv_ref[...], preferred_element_type=F32)
        if masked:
            acc_ref[qi_prev] = pv
        else:
            acc_ref[qi_prev] = alpha_ref[...] * acc_ref[qi_prev] + pv

        t = t_cur_ref[...]
        if masked:
            m_new = tmax_cur_ref[...]
            alpha_ref[...] = jnp.zeros_like(alpha_ref)
            p_cur_ref[...] = jnp.exp2(t - m_new).astype(BF16)
        else:
            m_old = m_ref[qi]
            shift = slope * ((j - qi) * T).astype(F32)
            m_new = jnp.maximum(m_old, tmax_cur_ref[...] + shift)
            alpha_ref[...] = jnp.exp2(m_old - m_new)
            p_cur_ref[...] = jnp.exp2(t + (shift - m_new)).astype(BF16)
        m_ref[qi] = m_new
        return qi_next, j_next, qi, j

    def steps(masked, last_next_masked, _, carry):
        for u in range(U):
            carry = step(carry, u, masked, masked if u + 1 < U else last_next_masked)
        return carry

    carry = (jnp.int32(0), jnp.int32(0), jnp.int32(dummy), jnp.int32(0))
    carry = lax.fori_loop(0, nblk // STEPS_PER_ITER - 1, functools.partial(steps, True, True), carry)
    carry = steps(True, False, 0, carry)
    carry = lax.fori_loop(0, n_off // STEPS_PER_ITER, functools.partial(steps, False, False), carry)
    _, _, qi_last, j_last = carry
    acc_ref[qi_last] = (alpha_ref[...] * acc_ref[qi_last]
                        + jnp.dot(vt_ref[j_last], bufs[U - 1][2][...], preferred_element_type=F32))

    gain = jnp.broadcast_to(subln_ref[...] * (1.0 - lam_init), (dv, T))
    for qi in range(nblk):
        acc = acc_ref[qi]
        r = 1.0 / acc[dv:dv + 1, :]
        out = acc[0:dv, :T] * r[:, :T] - acc[0:dv, T:] * (lam * r[:, T:])
        ms = jnp.mean(out * out, axis=0, keepdims=True)
        o_ref[0, qi * T:(qi + 1) * T, :] = (out * lax.rsqrt(ms + EPS) * gain).T.astype(BF16)


def _diff_attention(dqt, dk, dvt, diff_lambda, subln, *, heads, lam_init, cast_by_step=(), cast_by_row=(),
                    T=256):
    B, S, W = dk.shape
    hw = W // heads
    dh = hw // 2
    nblk = S // T
    assert T <= 256 and hw == LANES
    assert STEPS_PER_ITER % 2 == 0 and nblk % STEPS_PER_ITER == 0
    assert (nblk * (nblk - 1) // 2) % STEPS_PER_ITER == 0
    slopes = jnp.exp2(-8.0 * (jnp.arange(heads, dtype=F32) + 1.0) / heads) * LOG2E
    head_spec = pl.BlockSpec((1, S, hw), lambda b, h: (b, 0, h))
    head_t_spec = pl.BlockSpec((1, hw, S), lambda b, h: (b, h, 0))
    riders = ([_rider(w, B * heads, lambda b, h: b * heads + h) for w in cast_by_step]
              + [_rider(w, B, lambda b, h: b) for w in cast_by_row])
    return pl.pallas_call(
        functools.partial(_diff_kernel, T=T, seq=S, dh=dh, lam_init=lam_init,
                          n_step_riders=len(cast_by_step), n_row_riders=len(cast_by_row)),
        out_shape=[jax.ShapeDtypeStruct((B, S, W), BF16)] + [r[2] for r in riders],
        grid=(B, heads),
        in_specs=[head_t_spec, head_spec, head_t_spec,
                  _resident(diff_lambda.shape),
                  _resident((hw, 1)),
                  pl.BlockSpec((1, 1, 1), lambda b, h: (h, 0, 0))] + [r[0] for r in riders],
        out_specs=[head_spec] + [r[1] for r in riders],
        scratch_shapes=[pltpu.VMEM((nblk, hw + BF16_SUBLANES, T), BF16),
                        pltpu.VMEM((nblk, hw + LANES, 2 * T), BF16),
                        pltpu.VMEM((T, LANES), BF16),
                        pltpu.VMEM((nblk + 1, hw + BF16_SUBLANES, 2 * T), F32),
                        pltpu.VMEM((nblk, 1, 2 * T), F32),
                        pltpu.VMEM((1, 2 * T), F32)]
                       + [pltpu.VMEM((T, 2 * T), F32),
                          pltpu.VMEM((1, 2 * T), F32),
                          pltpu.VMEM((T, 2 * T), BF16)] * STEPS_PER_ITER,
        compiler_params=_params("arbitrary", "arbitrary"),
        name="diffattn",
    )(dqt, dk, dvt, diff_lambda, subln.reshape(hw, 1), slopes.reshape(heads, 1, 1),
      *cast_by_step, *cast_by_row)


def _mixout_kernel(x_ref, mod_ref, ret_ref, dif_ref, gs_ref, wro_ref, wdo_ref, wo_ref, o_ref):
    D = x_ref.shape[-1]
    y_ret = jnp.dot(ret_ref[...], wro_ref[...], preferred_element_type=F32)
    y_dif = jnp.dot(dif_ref[...], wdo_ref[...], preferred_element_type=F32)
    y = gs_ref[:, :D].astype(F32) * y_ret + gs_ref[:, D:].astype(F32) * y_dif
    gate = mod_ref[0, N_MOD * 1 + 2:N_MOD * 1 + 3, :]
    o_ref[...] = x_ref[...] + gate * jnp.dot(y.astype(BF16), wo_ref[...], preferred_element_type=F32)


def _mixout(x2d, mod3, ret2d, dif2d, gs2d, w_ro, w_do, w_o, *, seq, tm=512):
    M, D = x2d.shape
    tiles_per_seq = seq // tm
    row = lambda w: pl.BlockSpec((tm, w), lambda i: (i, 0))
    return pl.pallas_call(
        _mixout_kernel,
        out_shape=jax.ShapeDtypeStruct((M, D), F32),
        grid=(M // tm,),
        in_specs=[row(D),
                  pl.BlockSpec((1, N_SUB * N_MOD, D), lambda i: (i // tiles_per_seq, 0, 0)),
                  row(ret2d.shape[1]), row(dif2d.shape[1]), row(gs2d.shape[1]),
                  _resident(w_ro.shape), _resident(w_do.shape), _resident(w_o.shape)],
        out_specs=row(D),
        compiler_params=_params("arbitrary"),
        name="mixout",
    )(x2d, mod3, ret2d, dif2d, gs2d, w_ro, w_do, w_o)


def kernel(x, c, w_cond, b_cond, g_norm, w_ffn1_in, w_ffn1_out, w_in, w_ret_out, diff_lambda,
           diff_subln, w_diff_out, w_out, w_ffn2_in, w_ffn2_out, g_final):
    B, S, D = x.shape
    depth = w_cond.shape[0]
    diff_heads = D // (2 * DIFF_HEAD_DIM)
    xf = x.reshape(B * S, D)
    for l in range(depth):
        lam_init = 0.8 - 0.6 * math.exp(-0.3 * l)
        mod3 = _mod(c, w_cond[l], b_cond[l]).reshape(B, N_SUB * N_MOD, D)
        last = l == depth - 1

        xf, w_in_bf = _ffn(xf, mod3, g_norm[l], w_ffn1_in[l].astype(BF16), w_ffn1_out[l].astype(BF16),
                           None, sub=0, seq=S, to_cast=(w_in[l],))

        rq, rkt, rv, sg, dqt, dk, dvt, gs = _inproj(xf, mod3, g_norm[l], w_in_bf, batch=B, seq=S)
        b3 = lambda t: t.reshape(B, S, t.shape[-1])
        ret, w_ro_bf, w_do_bf, w_o_bf = _retention(
            b3(rq), rkt, b3(rv), b3(sg), heads=RET_HEADS, C=RET_CHUNK,
            to_cast=(w_ret_out[l], w_diff_out[l], w_out[l]))
        dif, w_f2i_bf, w_f2o_bf = _diff_attention(
            dqt, b3(dk), dvt, diff_lambda[l], diff_subln[l], heads=diff_heads, lam_init=lam_init,
            cast_by_step=(w_ffn2_in[l],), cast_by_row=(w_ffn2_out[l],))
        xf = _mixout(xf, mod3, ret.reshape(B * S, -1), dif.reshape(B * S, -1), gs,
                     w_ro_bf, w_do_bf, w_o_bf, seq=S)

        (xf,) = _ffn(xf, mod3, g_norm[l], w_f2i_bf, w_f2o_bf, g_final if last else None, sub=2, seq=S)
    if depth == 0:
        raise ValueError("depth must be positive")
    return xf.reshape(B, S, D)
```

```python
import functools
import math

import jax
import jax.numpy as jnp
from jax import lax
from jax.experimental import pallas as pl
from jax.experimental.pallas import tpu as pltpu

F32 = jnp.float32
BF16 = jnp.bfloat16
EPS = 1e-6
NEG = -1e30
LOG2E = math.log2(math.e)

RET_HEADS = 4
RET_CHUNK = 256
RET_CHUNKS_PER_STEP = 4
DIFF_HEAD_DIM = 64
N_SUB = 3
N_MOD = 3

VMEM_LIMIT_BYTES = 56 * 1024 * 1024
BF16_SUBLANES = 16
LANES = 128


def _resident(shape):
    nd = len(shape)
    return pl.BlockSpec(shape, lambda *_: (0,) * nd, pipeline_mode=pl.Buffered(1))


def _params(*sem):
    return pltpu.CompilerParams(dimension_semantics=sem, vmem_limit_bytes=VMEM_LIMIT_BYTES)


def _silu(a):
    return a / (1.0 + jnp.exp(-a))


def _sigmoid(a):
    return 1.0 / (1.0 + jnp.exp(-a))


def _modulated(x, mod_ref, g_ref, sub):
    shift = mod_ref[0, N_MOD * sub:N_MOD * sub + 1, :]
    scale = mod_ref[0, N_MOD * sub + 1:N_MOD * sub + 2, :]
    ms = jnp.mean(x * x, axis=-1, keepdims=True)
    h = x * lax.rsqrt(ms + EPS) * g_ref[sub:sub + 1, :]
    return (h * (1.0 + scale) + shift).astype(BF16)


def _rider(w, nblocks, block_of_step):
    rows = w.shape[0] // nblocks
    assert rows * nblocks == w.shape[0] and rows % BF16_SUBLANES == 0
    spec = pl.BlockSpec((rows, w.shape[1]), lambda *g: (block_of_step(*g), 0))
    return spec, spec, jax.ShapeDtypeStruct(w.shape, BF16)


def _run_riders(in_refs, out_refs):
    for src, dst in zip(in_refs, out_refs):
        dst[...] = src[...].astype(BF16)


def _mod_kernel(c_ref, w_ref, b_ref, o_ref):
    c_act = _silu(c_ref[...]).astype(BF16)
    o_ref[...] = jnp.dot(c_act, w_ref[...].astype(BF16), preferred_element_type=F32) + b_ref[...]


def _mod(c, w, b):
    B, D = c.shape
    N = w.shape[1]
    tn = 2304
    assert N % tn == 0
    return pl.pallas_call(
        _mod_kernel,
        out_shape=jax.ShapeDtypeStruct((B, N), F32),
        grid=(N // tn,),
        in_specs=[pl.BlockSpec((B, D), lambda j: (0, 0)),
                  pl.BlockSpec((D, tn), lambda j: (0, j)),
                  pl.BlockSpec((1, tn), lambda j: (0, j))],
        out_specs=pl.BlockSpec((B, tn), lambda j: (0, j)),
        compiler_params=_params("arbitrary"),
        name="mod",
    )(c, w, b.reshape(1, N))


def _ffn_kernel(x_ref, mod_ref, g_ref, win_ref, wout_ref, *rest, sub, d_ff, tf, final, n_riders):
    rest = list(rest)
    gfin_ref = rest.pop(0) if final else None
    rider_in = [rest.pop(0) for _ in range(n_riders)]
    o_ref = rest.pop(0)
    rider_out = [rest.pop(0) for _ in range(n_riders)]
    (act_ref,) = rest
    _run_riders(rider_in, rider_out)
    x = x_ref[...]
    h = _modulated(x, mod_ref, g_ref, sub)
    for c in range(d_ff // tf):
        a = jnp.dot(h, win_ref[:, c * tf:(c + 1) * tf], preferred_element_type=F32)
        b = jnp.dot(h, win_ref[:, d_ff + c * tf:d_ff + (c + 1) * tf], preferred_element_type=F32)
        act_ref[:, c * tf:(c + 1) * tf] = (_silu(a) * b).astype(BF16)
    y = jnp.dot(act_ref[...], wout_ref[...], preferred_element_type=F32)
    gate = mod_ref[0, N_MOD * sub + 2:N_MOD * sub + 3, :]
    out = x + (0.5 * gate) * y
    if final:
        ms = jnp.mean(out * out, axis=-1, keepdims=True)
        out = out * lax.rsqrt(ms + EPS) * gfin_ref[...]
    o_ref[...] = out


def _ffn(x2d, mod3, g_norm, w_in, w_out, g_final, *, sub, seq, to_cast=(), tm=512, tf=256):
    M, D = x2d.shape
    d_ff = w_out.shape[0]
    tiles_per_seq = seq // tm
    final = g_final is not None
    in_specs = [pl.BlockSpec((tm, D), lambda i: (i, 0)),
                pl.BlockSpec((1, N_SUB * N_MOD, D), lambda i: (i // tiles_per_seq, 0, 0)),
                _resident(g_norm.shape),
                _resident(w_in.shape),
                _resident(w_out.shape)]
    args = [x2d, mod3, g_norm, w_in, w_out]
    if final:
        in_specs.append(_resident((1, D)))
        args.append(g_final.reshape(1, D))
    riders = [_rider(w, M // tm, lambda i: i) for w in to_cast]
    return pl.pallas_call(
        functools.partial(_ffn_kernel, sub=sub, d_ff=d_ff, tf=tf, final=final, n_riders=len(riders)),
        out_shape=[jax.ShapeDtypeStruct((M, D), F32)] + [r[2] for r in riders],
        grid=(M // tm,),
        in_specs=in_specs + [r[0] for r in riders],
        out_specs=[pl.BlockSpec((tm, D), lambda i: (i, 0))] + [r[1] for r in riders],
        scratch_shapes=[pltpu.VMEM((tm, d_ff), BF16)],
        compiler_params=_params("arbitrary"),
        name="ffn%d" % sub,
    )(*args, *to_cast)


def _inproj_segments(D):
    rqk, rv = D, 2 * D
    return (("rq", rqk, None, False), ("rk", rqk, "ret_k_scale", True), ("rv", rv, None, False),
            ("rg", rv, "silu", False), ("dq", D, "diff_q_scale", True), ("dk", D, None, False),
            ("dv", D, None, True), ("gates", 2 * D, "sigmoid", False))


def _inproj_kernel(x_ref, mod_ref, g_ref, w_ref, *out_refs, segs, tn, ret_k_scale, diff_q_scale):
    h = _modulated(x_ref[...], mod_ref, g_ref, 1)
    col = 0
    for (_, width, epi, transposed), o_ref in zip(segs, out_refs):
        for c in range(width // tn):
            y = jnp.dot(h, w_ref[:, col + c * tn:col + (c + 1) * tn], preferred_element_type=F32)
            if epi == "silu":
                y = _silu(y)
            elif epi == "sigmoid":
                y = _sigmoid(y)
            elif epi == "ret_k_scale":
                y = y * ret_k_scale
            elif epi == "diff_q_scale":
                y = y * diff_q_scale
            if transposed:
                o_ref[0, c * tn:(c + 1) * tn, :] = y.astype(BF16).T
            else:
                o_ref[:, c * tn:(c + 1) * tn] = y.astype(BF16)
        col += width


def _inproj(x2d, mod3, g_norm, w_in, *, batch, seq, tm=256, tn=512):
    M, D = x2d.shape
    segs = _inproj_segments(D)
    assert sum(s[1] for s in segs) == w_in.shape[1]
    tiles_per_seq = seq // tm
    kern = functools.partial(
        _inproj_kernel, segs=segs, tn=tn,
        ret_k_scale=float((D // RET_HEADS) ** -0.5),
        diff_q_scale=LOG2E * DIFF_HEAD_DIM ** -0.5)
    out_shape, out_specs = [], []
    for _, width, _, transposed in segs:
        if transposed:
            out_shape.append(jax.ShapeDtypeStruct((batch, width, seq), BF16))
            out_specs.append(pl.BlockSpec((1, width, tm),
                                          lambda i: (i // tiles_per_seq, 0, i % tiles_per_seq)))
        else:
            out_shape.append(jax.ShapeDtypeStruct((M, width), BF16))
            out_specs.append(pl.BlockSpec((tm, width), lambda i: (i, 0)))
    return pl.pallas_call(
        kern,
        out_shape=out_shape,
        grid=(M // tm,),
        in_specs=[pl.BlockSpec((tm, D), lambda i: (i, 0)),
                  pl.BlockSpec((1, N_SUB * N_MOD, D), lambda i: (i // tiles_per_seq, 0, 0)),
                  _resident(g_norm.shape),
                  _resident(w_in.shape)],
        out_specs=out_specs,
        compiler_params=_params("arbitrary"),
        name="inproj",
    )(x2d, mod3, g_norm, w_in)


def _ret_kernel(q_ref, kt_ref, v_ref, sg_ref, dec_ref, xi_ref, zeta_ref, gch_ref, *rest,
                heads, dk, dv, n_riders):
    rider_in, (o_ref, *rider_out), state_ref = rest[:n_riders], rest[n_riders:-1], rest[-1]
    _run_riders(rider_in, rider_out)

    @pl.when(pl.program_id(1) == 0)
    def _():
        state_ref[...] = jnp.zeros_like(state_ref)

    C = dec_ref.shape[-1]

    def chunk(rows):
        q = [q_ref[0, rows, h * dk:(h + 1) * dk] for h in range(heads)]
        kt = [kt_ref[0, h * dk:(h + 1) * dk, rows] for h in range(heads)]
        v = [v_ref[0, rows, h * dv:(h + 1) * dv] for h in range(heads)]

        def first_matmuls(h):
            s = jnp.dot(q[h], kt[h], preferred_element_type=F32)
            cross = jnp.dot(q[h], state_ref[h].astype(BF16), preferred_element_type=F32)
            return s, cross

        def stacked_lhs(h, s):
            intra = (s * dec_ref[h]).astype(BF16)
            kz_t = (kt[h].astype(F32) * zeta_ref[h]).astype(BF16)
            return jnp.concatenate([intra, kz_t], axis=0)

        def finish(h, both, cross):
            state_ref[h] = state_ref[h] * gch_ref[h] + both[C:, :]
            xi = jnp.concatenate([xi_ref[h]] * (dv // LANES), axis=1)
            o = both[:C, :] + cross * xi
            mu = jnp.mean(o, axis=-1, keepdims=True)
            d = o - mu
            var = jnp.mean(d * d, axis=-1, keepdims=True)
            on = d * lax.rsqrt(var + EPS)
            gate = sg_ref[0, rows, h * dv:(h + 1) * dv].astype(F32)
            o_ref[0, rows, h * dv:(h + 1) * dv] = (on * gate).astype(BF16)

        for h0 in range(0, heads, 2):
            pair = range(h0, min(h0 + 2, heads))
            first = [first_matmuls(h) for h in pair]
            both = [jnp.dot(stacked_lhs(h, s), v[h], preferred_element_type=F32)
                    for h, (s, _) in zip(pair, first)]
            for h, b, (_, cross) in zip(pair, both, first):
                finish(h, b, cross)

    for c in range(q_ref.shape[1] // C):
        chunk(slice(c * C, (c + 1) * C))


def _retention_tables(heads, C):
    hh = jnp.arange(heads, dtype=F32)
    log_g = jnp.log1p(-jnp.exp2(-5.0 - hh))
    pos = jnp.arange(C, dtype=F32)
    rel = pos[:, None] - pos[None, :]
    decay = jnp.where(rel[None] >= 0, jnp.exp(jnp.maximum(rel, 0.0)[None] * log_g[:, None, None]), 0.0)
    xi = jnp.exp((pos + 1.0)[None, :] * log_g[:, None])[..., None]
    zeta = jnp.exp((C - 1.0 - pos)[None, :] * log_g[:, None])[:, None, :]
    g_chunk = jnp.exp(C * log_g)[:, None, None]
    return decay, jnp.broadcast_to(xi, (heads, C, LANES)), zeta, g_chunk


def _retention(rq, rkt, rv, sg, *, heads, C, to_cast=()):
    B, S, qk_w = rq.shape
    v_w = rv.shape[-1]
    dk, dv = qk_w // heads, v_w // heads
    span = RET_CHUNKS_PER_STEP * C
    nsteps = S // span
    assert nsteps * span == S
    decay, xi, zeta, g_chunk = _retention_tables(heads, C)
    q_spec = pl.BlockSpec((1, span, qk_w), lambda b, n: (b, n, 0))
    kt_spec = pl.BlockSpec((1, qk_w, span), lambda b, n: (b, 0, n))
    v_spec = pl.BlockSpec((1, span, v_w), lambda b, n: (b, n, 0))
    riders = [_rider(w, B * nsteps, lambda b, n: b * nsteps + n) for w in to_cast]
    return pl.pallas_call(
        functools.partial(_ret_kernel, heads=heads, dk=dk, dv=dv, n_riders=len(riders)),
        out_shape=[jax.ShapeDtypeStruct((B, S, v_w), BF16)] + [r[2] for r in riders],
        grid=(B, nsteps),
        in_specs=[q_spec, kt_spec, v_spec, v_spec,
                  _resident(decay.shape), _resident(xi.shape), _resident(zeta.shape),
                  _resident(g_chunk.shape)] + [r[0] for r in riders],
        out_specs=[v_spec] + [r[1] for r in riders],
        scratch_shapes=[pltpu.VMEM((heads, dk, dv), F32)],
        compiler_params=_params("arbitrary", "arbitrary"),
        name="retention",
    )(rq, rkt, rv, sg, decay, xi, zeta, g_chunk, *to_cast)


STEPS_PER_ITER = 4
DIFF_HEADS_PER_STEP = 2


def _diff_kernel(qt_ref, k_ref, vt_in_ref, lam_ref, subln_ref, slope_ref, *rest,
                 T, seq, dh, lam_init, n_step_riders, n_row_riders):
    n_riders = n_step_riders + n_row_riders
    rider_in, o_ref, rider_out = rest[:n_riders], rest[n_riders], rest[n_riders + 1:2 * n_riders + 1]
    scratch = rest[2 * n_riders + 1:]
    _run_riders(rider_in[:n_step_riders], rider_out[:n_step_riders])

    @pl.when(pl.program_id(1) == 0)
    def _():
        _run_riders(rider_in[n_step_riders:], rider_out[n_step_riders:])

    lp = lam_ref[...]
    lam = (jnp.exp(jnp.sum(lp[0:1] * lp[1:2], axis=-1, keepdims=True))
           - jnp.exp(jnp.sum(lp[2:3] * lp[3:4], axis=-1, keepdims=True)) + lam_init)
    gain = subln_ref[...] * (1.0 - lam_init)
    hw = 2 * dh
    for hh in range(DIFF_HEADS_PER_STEP):
        cols = slice(hh * hw, (hh + 1) * hw)
        _diff_head(qt_ref.at[0, cols, :], k_ref.at[0, :, cols], vt_in_ref.at[0, cols, :], slope_ref[hh],
                   lam, gain, o_ref.at[0, :, cols], *scratch, T=T, seq=seq, dh=dh)


def _diff_head(qt_ref, k_ref, vt_in_ref, slope, lam, gain, o_ref,
               vt_ref, wq_ref, kaug_ref, acc_ref, m_ref, alpha_ref, *buf_refs, T, seq, dh):
    nblk = seq // T
    n_off = nblk * (nblk - 1) // 2
    dv = 2 * dh
    dv_ext = dv + BF16_SUBLANES
    dummy = nblk

    row = lax.broadcasted_iota(jnp.int32, (T, LANES), 0)
    lane = lax.broadcasted_iota(jnp.int32, (T, LANES), 1)
    kaug_ref[...] = jnp.where(lane < 3, row, 0).astype(F32).astype(BF16)
    s_hi = slope.astype(BF16).astype(F32)
    s_mid = (slope - s_hi).astype(BF16).astype(F32)
    s_lo = slope - s_hi - s_mid
    aug_row = lax.broadcasted_iota(jnp.int32, (LANES, 2 * T), 0)
    slope_rows = jnp.where(aug_row == 0, s_hi, jnp.where(aug_row == 1, s_mid,
                           jnp.where(aug_row == 2, s_lo, 0.0))).astype(BF16)
    ones_rows = jnp.where(lax.broadcasted_iota(jnp.int32, (BF16_SUBLANES, T), 0) == 0, 1.0, 0.0).astype(BF16)
    feat_id = lax.broadcasted_iota(jnp.int32, (dv, T), 0)
    for j in range(nblk):
        cols = slice(j * T, (j + 1) * T)
        vt_ref[j, 0:dv, :] = vt_in_ref[:, cols]
        vt_ref[j, dv:dv_ext, :] = ones_rows
        qt = qt_ref[:, cols]
        zero = jnp.zeros_like(qt)
        wq_ref[j, 0:dv, :] = jnp.concatenate([jnp.where(feat_id < dh, qt, zero),
                                              jnp.where(feat_id >= dh, qt, zero)], axis=1)
        wq_ref[j, dv:dv + LANES, :] = slope_rows

    def scores(qi, j):
        k0 = j * T if isinstance(j, int) else pl.multiple_of(j * T, T)
        k_blk = k_ref[pl.ds(k0, T), :]
        return jnp.dot(jnp.concatenate([k_blk, kaug_ref[...]], axis=1), wq_ref[qi],
                       preferred_element_type=F32)

    key_id = lax.broadcasted_iota(jnp.int32, (T, 2 * T), 0)
    col_id = lax.broadcasted_iota(jnp.int32, (T, 2 * T), 1)
    causal = key_id <= jnp.where(col_id >= T, col_id - T, col_id)

    acc_ref[nblk - 1] = jnp.zeros((dv_ext, 2 * T), F32)

    U = STEPS_PER_ITER
    bufs = [buf_refs[3 * u:3 * u + 3] for u in range(U)]
    bufs[U - 1][2][...] = jnp.zeros((T, 2 * T), BF16)

    def store_scores(t_ref, tmax_ref, t, masked):
        if masked:
            t = jnp.where(causal, t, NEG)
        t_ref[...] = t
        tmax_ref[...] = jnp.max(t, axis=0, keepdims=True)

    store_scores(bufs[0][0], bufs[0][1], scores(0, 0), True)

    def next_block(qi, j):
        on_diag = j == qi
        last_diag = qi == nblk - 1
        end_of_tile = j == qi - 1
        qi_next = jnp.where(on_diag, jnp.where(last_diag, 1, qi + 1), jnp.where(end_of_tile, qi + 1, qi))
        j_next = jnp.where(on_diag, jnp.where(last_diag, 0, qi + 1), jnp.where(end_of_tile, 0, j + 1))
        return qi_next, j_next

    def step(carry, u, masked, next_masked):
        t_cur_ref, tmax_cur_ref, p_cur_ref = bufs[u]
        t_nxt_ref, tmax_nxt_ref, _ = bufs[(u + 1) % U]
        p_prv_ref = bufs[(u - 1) % U][2]
        qi, j, qi_prev, j_prev = carry
        qi_next, j_next = next_block(qi, j)

        store_scores(t_nxt_ref, tmax_nxt_ref, scores(jnp.minimum(qi_next, nblk - 1), j_next), next_masked)

        pv = jnp.dot(vt_ref[j_prev], p_prv_ref[...], preferred_element_type=F32)
        if masked:
            acc_ref[qi_prev] = pv
        else:
            acc_ref[qi_prev] = alpha_ref[...] * acc_ref[qi_prev] + pv

        t = t_cur_ref[...]
        if masked:
            m_new = tmax_cur_ref[...]
            alpha_ref[...] = jnp.zeros_like(alpha_ref)
            p_cur_ref[...] = jnp.exp2(t - m_new).astype(BF16)
        else:
            m_old = m_ref[qi]
            shift = slope * ((j - qi) * T).astype(F32)
            m_new = jnp.maximum(m_old, tmax_cur_ref[...] + shift)
            alpha_ref[...] = jnp.exp2(m_old - m_new)
            p_cur_ref[...] = jnp.exp2(t + (shift - m_new)).astype(BF16)
        m_ref[qi] = m_new
        return qi_next, j_next, qi, j

    def steps(masked, last_next_masked, _, carry):
        for u in range(U):
            carry = step(carry, u, masked, masked if u + 1 < U else last_next_masked)
        return carry

    carry = (jnp.int32(0), jnp.int32(0), jnp.int32(dummy), jnp.int32(0))
    carry = lax.fori_loop(0, nblk // STEPS_PER_ITER - 1, functools.partial(steps, True, True), carry)
    carry = steps(True, False, 0, carry)
    carry = lax.fori_loop(0, n_off // STEPS_PER_ITER, functools.partial(steps, False, False), carry)
    _, _, qi_last, j_last = carry
    acc_ref[qi_last] = (alpha_ref[...] * acc_ref[qi_last]
                        + jnp.dot(vt_ref[j_last], bufs[U - 1][2][...], preferred_element_type=F32))

    gain = jnp.broadcast_to(gain, (dv, T))
    for qi in range(nblk):
        acc = acc_ref[qi]
        r = 1.0 / acc[dv:dv + 1, :]
        out = acc[0:dv, :T] * r[:, :T] - acc[0:dv, T:] * (lam * r[:, T:])
        ms = jnp.mean(out * out, axis=0, keepdims=True)
        o_ref[qi * T:(qi + 1) * T, :] = (out * lax.rsqrt(ms + EPS) * gain).T.astype(BF16)


def _diff_attention(dqt, dk, dvt, diff_lambda, subln, *, heads, lam_init, cast_by_step=(), cast_by_row=(),
                    T=256):
    B, S, W = dk.shape
    hw = W // heads
    dh = hw // 2
    nblk = S // T
    assert T <= 256 and hw == LANES
    assert STEPS_PER_ITER % 2 == 0 and nblk % STEPS_PER_ITER == 0
    assert (nblk * (nblk - 1) // 2) % STEPS_PER_ITER == 0
    slopes = jnp.exp2(-8.0 * (jnp.arange(heads, dtype=F32) + 1.0) / heads) * LOG2E
    hps = DIFF_HEADS_PER_STEP
    groups = heads // hps
    assert groups * hps == heads
    head_spec = pl.BlockSpec((1, S, hps * hw), lambda b, g: (b, 0, g))
    head_t_spec = pl.BlockSpec((1, hps * hw, S), lambda b, g: (b, g, 0))
    riders = ([_rider(w, B * groups, lambda b, g: b * groups + g) for w in cast_by_step]
              + [_rider(w, B, lambda b, g: b) for w in cast_by_row])
    return pl.pallas_call(
        functools.partial(_diff_kernel, T=T, seq=S, dh=dh, lam_init=lam_init,
                          n_step_riders=len(cast_by_step), n_row_riders=len(cast_by_row)),
        out_shape=[jax.ShapeDtypeStruct((B, S, W), BF16)] + [r[2] for r in riders],
        grid=(B, groups),
        in_specs=[head_t_spec, head_spec, head_t_spec,
                  _resident(diff_lambda.shape),
                  _resident((hw, 1)),
                  pl.BlockSpec((hps, 1, 1), lambda b, g: (g, 0, 0))] + [r[0] for r in riders],
        out_specs=[head_spec] + [r[1] for r in riders],
        scratch_shapes=[pltpu.VMEM((nblk, hw + BF16_SUBLANES, T), BF16),
                        pltpu.VMEM((nblk, hw + LANES, 2 * T), BF16),
                        pltpu.VMEM((T, LANES), BF16),
                        pltpu.VMEM((nblk + 1, hw + BF16_SUBLANES, 2 * T), F32),
                        pltpu.VMEM((nblk, 1, 2 * T), F32),
                        pltpu.VMEM((1, 2 * T), F32)]
                       + [pltpu.VMEM((T, 2 * T), F32),
                          pltpu.VMEM((1, 2 * T), F32),
                          pltpu.VMEM((T, 2 * T), BF16)] * STEPS_PER_ITER,
        compiler_params=_params("arbitrary", "arbitrary"),
        name="diffattn",
    )(dqt, dk, dvt, diff_lambda, subln.reshape(hw, 1), slopes.reshape(heads, 1, 1),
      *cast_by_step, *cast_by_row)


def _mixout_kernel(x_ref, mod_ref, ret_ref, dif_ref, gs_ref, wro_ref, wdo_ref, wo_ref, o_ref):
    D = x_ref.shape[-1]
    y_ret = jnp.dot(ret_ref[...], wro_ref[...], preferred_element_type=F32)
    y_dif = jnp.dot(dif_ref[...], wdo_ref[...], preferred_element_type=F32)
    y = gs_ref[:, :D].astype(F32) * y_ret + gs_ref[:, D:].astype(F32) * y_dif
    gate = mod_ref[0, N_MOD * 1 + 2:N_MOD * 1 + 3, :]
    o_ref[...] = x_ref[...] + gate * jnp.dot(y.astype(BF16), wo_ref[...], preferred_element_type=F32)


def _mixout(x2d, mod3, ret2d, dif2d, gs2d, w_ro, w_do, w_o, *, seq, tm=512):
    M, D = x2d.shape
    tiles_per_seq = seq // tm
    row = lambda w: pl.BlockSpec((tm, w), lambda i: (i, 0))
    return pl.pallas_call(
        _mixout_kernel,
        out_shape=jax.ShapeDtypeStruct((M, D), F32),
        grid=(M // tm,),
        in_specs=[row(D),
                  pl.BlockSpec((1, N_SUB * N_MOD, D), lambda i: (i // tiles_per_seq, 0, 0)),
                  row(ret2d.shape[1]), row(dif2d.shape[1]), row(gs2d.shape[1]),
                  _resident(w_ro.shape), _resident(w_do.shape), _resident(w_o.shape)],
        out_specs=row(D),
        compiler_params=_params("arbitrary"),
        name="mixout",
    )(x2d, mod3, ret2d, dif2d, gs2d, w_ro, w_do, w_o)


def kernel(x, c, w_cond, b_cond, g_norm, w_ffn1_in, w_ffn1_out, w_in, w_ret_out, diff_lambda,
           diff_subln, w_diff_out, w_out, w_ffn2_in, w_ffn2_out, g_final):
    B, S, D = x.shape
    depth = w_cond.shape[0]
    diff_heads = D // (2 * DIFF_HEAD_DIM)
    xf = x.reshape(B * S, D)
    for l in range(depth):
        lam_init = 0.8 - 0.6 * math.exp(-0.3 * l)
        mod3 = _mod(c, w_cond[l], b_cond[l]).reshape(B, N_SUB * N_MOD, D)
        last = l == depth - 1

        xf, w_in_bf = _ffn(xf, mod3, g_norm[l], w_ffn1_in[l].astype(BF16), w_ffn1_out[l].astype(BF16),
                           None, sub=0, seq=S, to_cast=(w_in[l],))

        rq, rkt, rv, sg, dqt, dk, dvt, gs = _inproj(xf, mod3, g_norm[l], w_in_bf, batch=B, seq=S)
        b3 = lambda t: t.reshape(B, S, t.shape[-1])
        ret, w_ro_bf, w_do_bf, w_o_bf = _retention(
            b3(rq), rkt, b3(rv), b3(sg), heads=RET_HEADS, C=RET_CHUNK,
            to_cast=(w_ret_out[l], w_diff_out[l], w_out[l]))
        dif, w_f2i_bf, w_f2o_bf = _diff_attention(
            dqt, b3(dk), dvt, diff_lambda[l], diff_subln[l], heads=diff_heads, lam_init=lam_init,
            cast_by_step=(w_ffn2_in[l],), cast_by_row=(w_ffn2_out[l],))
        xf = _mixout(xf, mod3, ret.reshape(B * S, -1), dif.reshape(B * S, -1), gs,
                     w_ro_bf, w_do_bf, w_o_bf, seq=S)

        (xf,) = _ffn(xf, mod3, g_norm[l], w_f2i_bf, w_f2o_bf, g_final if last else None, sub=2, seq=S)
    if depth == 0:
        raise ValueError("depth must be positive")
    return xf.reshape(B, S, D)
```

```python
import functools
import math

import jax
import jax.numpy as jnp
from jax import lax
from jax.experimental import pallas as pl
from jax.experimental.pallas import tpu as pltpu

F32 = jnp.float32
BF16 = jnp.bfloat16
EPS = 1e-6
NEG = -1e30
LOG2E = math.log2(math.e)

RET_HEADS = 4
RET_CHUNK = 256
DIFF_HEAD_DIM = 64
N_SUB = 3
N_MOD = 3

VMEM_LIMIT_BYTES = 56 * 1024 * 1024
BF16_SUBLANES = 16
LANES = 128


def _resident(shape):
    nd = len(shape)
    return pl.BlockSpec(shape, lambda *_: (0,) * nd, pipeline_mode=pl.Buffered(1))


def _params(*sem):
    return pltpu.CompilerParams(dimension_semantics=sem, vmem_limit_bytes=VMEM_LIMIT_BYTES)


def _silu(a):
    return a / (1.0 + jnp.exp(-a))


def _sigmoid(a):
    return 1.0 / (1.0 + jnp.exp(-a))


def _modulated(x, mod_ref, g_ref, sub):
    shift = mod_ref[0, N_MOD * sub:N_MOD * sub + 1, :]
    scale = mod_ref[0, N_MOD * sub + 1:N_MOD * sub + 2, :]
    ms = jnp.mean(x * x, axis=-1, keepdims=True)
    gain = g_ref[sub:sub + 1, :] * (1.0 + scale)
    return (x * lax.rsqrt(ms + EPS) * gain + shift).astype(BF16)


def _rider(w, nblocks, block_of_step):
    rows = w.shape[0] // nblocks
    assert rows * nblocks == w.shape[0] and rows % BF16_SUBLANES == 0
    spec = pl.BlockSpec((rows, w.shape[1]), lambda *g: (block_of_step(*g), 0))
    return spec, spec, jax.ShapeDtypeStruct(w.shape, BF16)


def _run_riders(in_refs, out_refs):
    for src, dst in zip(in_refs, out_refs):
        dst[...] = src[...].astype(BF16)


def _mod_kernel(c_ref, w_ref, b_ref, o_ref):
    c_act = _silu(c_ref[...]).astype(BF16)
    o_ref[...] = jnp.dot(c_act, w_ref[...].astype(BF16), preferred_element_type=F32) + b_ref[...]


def _mod(c, w, b):
    B, D = c.shape
    N = w.shape[1]
    tn = 2304
    assert N % tn == 0
    return pl.pallas_call(
        _mod_kernel,
        out_shape=jax.ShapeDtypeStruct((B, N), F32),
        grid=(N // tn,),
        in_specs=[pl.BlockSpec((B, D), lambda j: (0, 0)),
                  pl.BlockSpec((D, tn), lambda j: (0, j)),
                  pl.BlockSpec((1, tn), lambda j: (0, j))],
        out_specs=pl.BlockSpec((B, tn), lambda j: (0, j)),
        compiler_params=_params("arbitrary"),
        name="mod",
    )(c, w, b.reshape(1, N))


def _ffn_kernel(x_ref, mod_ref, g_ref, win_ref, wout_ref, *rest, sub, d_ff, tf, final, n_riders):
    rest = list(rest)
    gfin_ref = rest.pop(0) if final else None
    rider_in = [rest.pop(0) for _ in range(n_riders)]
    o_ref = rest.pop(0)
    rider_out = [rest.pop(0) for _ in range(n_riders)]
    (act_ref,) = rest
    _run_riders(rider_in, rider_out)
    x = x_ref[...]
    h = _modulated(x, mod_ref, g_ref, sub)
    for c in range(d_ff // tf):
        a = jnp.dot(h, win_ref[:, c * tf:(c + 1) * tf], preferred_element_type=F32)
        b = jnp.dot(h, win_ref[:, d_ff + c * tf:d_ff + (c + 1) * tf], preferred_element_type=F32)
        act_ref[:, c * tf:(c + 1) * tf] = (_silu(a) * b).astype(BF16)
    y = jnp.dot(act_ref[...], wout_ref[...], preferred_element_type=F32)
    gate = mod_ref[0, N_MOD * sub + 2:N_MOD * sub + 3, :]
    out = x + (0.5 * gate) * y
    if final:
        ms = jnp.mean(out * out, axis=-1, keepdims=True)
        out = out * lax.rsqrt(ms + EPS) * gfin_ref[...]
    o_ref[...] = out


def _ffn(x2d, mod3, g_norm, w_in, w_out, g_final, *, sub, seq, to_cast=(), tm=512, tf=256):
    M, D = x2d.shape
    d_ff = w_out.shape[0]
    tiles_per_seq = seq // tm
    final = g_final is not None
    in_specs = [pl.BlockSpec((tm, D), lambda i: (i, 0)),
                pl.BlockSpec((1, N_SUB * N_MOD, D), lambda i: (i // tiles_per_seq, 0, 0)),
                _resident(g_norm.shape),
                _resident(w_in.shape),
                _resident(w_out.shape)]
    args = [x2d, mod3, g_norm, w_in, w_out]
    if final:
        in_specs.append(_resident((1, D)))
        args.append(g_final.reshape(1, D))
    riders = [_rider(w, M // tm, lambda i: i) for w in to_cast]
    return pl.pallas_call(
        functools.partial(_ffn_kernel, sub=sub, d_ff=d_ff, tf=tf, final=final, n_riders=len(riders)),
        out_shape=[jax.ShapeDtypeStruct((M, D), F32)] + [r[2] for r in riders],
        grid=(M // tm,),
        in_specs=in_specs + [r[0] for r in riders],
        out_specs=[pl.BlockSpec((tm, D), lambda i: (i, 0))] + [r[1] for r in riders],
        scratch_shapes=[pltpu.VMEM((tm, d_ff), BF16)],
        compiler_params=_params("arbitrary"),
        name="ffn%d" % sub,
    )(*args, *to_cast)


def _inproj_segments(D):
    return (("rq", D, None, "ret"), ("rk", D, "ret_k_scale", "ret_t"), ("rv", 2 * D, None, "ret"),
            ("rg", 2 * D, "silu", "ret"), ("dq", D, "diff_q_scale", "out_t"), ("dk", D, None, "out"),
            ("dv", D, None, "out_t"), ("gates", 2 * D, "sigmoid", "out"))


def _inproj_kernel(x_ref, mod_ref, g_ref, w_ref, dec_ref, xi_ref, zeta_ref, gch_ref, *rest,
                   segs, tn, ret_k_scale, diff_q_scale, tiles_per_seq, ret_heads, n_riders):
    rest = list(rest)
    rider_in = [rest.pop(0) for _ in range(n_riders)]
    ret_ref = rest.pop(0)
    dest = {s[0]: rest.pop(0) for s in segs if s[3].startswith("out")}
    rider_out = [rest.pop(0) for _ in range(n_riders)]
    dest["rq"], dest["rk"], dest["rv"], dest["rg"], state_ref = rest
    _run_riders(rider_in, rider_out)

    @pl.when(lax.rem(pl.program_id(0), tiles_per_seq) == 0)
    def _():
        state_ref[...] = jnp.zeros_like(state_ref)

    h = _modulated(x_ref[...], mod_ref, g_ref, 1)
    col = 0
    retention_stages = iter(())
    for name, width, epi, kind in segs:
        o_ref = dest[name]
        for c in range(width // tn):
            next(retention_stages, None)
            y = jnp.dot(h, w_ref[:, col + c * tn:col + (c + 1) * tn], preferred_element_type=F32)
            if epi == "silu":
                y = _silu(y)
            elif epi == "sigmoid":
                y = _sigmoid(y)
            elif epi == "ret_k_scale":
                y = y * ret_k_scale
            elif epi == "diff_q_scale":
                y = y * diff_q_scale
            if kind == "out_t":
                o_ref[0, c * tn:(c + 1) * tn, :] = y.astype(BF16).T
            elif kind == "ret_t":
                o_ref[c * tn:(c + 1) * tn, :] = y.astype(BF16).T
            else:
                o_ref[:, c * tn:(c + 1) * tn] = y.astype(BF16)
        col += width
        if name == "rg":
            qk_w, v_w = dest["rq"].shape[1], dest["rv"].shape[1]
            retention_stages = _retention_chunk(
                dest["rq"], dest["rk"], dest["rv"], dest["rg"], ret_ref,
                dec_ref, xi_ref, zeta_ref, gch_ref, state_ref,
                heads=ret_heads, dk=qk_w // ret_heads, dv=v_w // ret_heads)
    for _ in retention_stages:
        pass


def _inproj(x2d, mod3, g_norm, w_in, *, batch, seq, to_cast=(), tn=512):
    M, D = x2d.shape
    tm = RET_CHUNK
    segs = _inproj_segments(D)
    assert sum(s[1] for s in segs) == w_in.shape[1]
    tiles_per_seq = seq // tm
    widths = {s[0]: s[1] for s in segs}
    decay, xi, zeta, g_chunk = _retention_tables(RET_HEADS, tm)
    kern = functools.partial(
        _inproj_kernel, segs=segs, tn=tn, tiles_per_seq=tiles_per_seq, ret_heads=RET_HEADS,
        ret_k_scale=float((D // RET_HEADS) ** -0.5),
        diff_q_scale=LOG2E * DIFF_HEAD_DIM ** -0.5,
        n_riders=len(to_cast))
    row_spec = lambda width: pl.BlockSpec((tm, width), lambda i: (i, 0))
    out_shape, out_specs = [jax.ShapeDtypeStruct((M, widths["rv"]), BF16)], [row_spec(widths["rv"])]
    for _, width, _, kind in segs:
        if kind == "out_t":
            out_shape.append(jax.ShapeDtypeStruct((batch, width, seq), BF16))
            out_specs.append(pl.BlockSpec((1, width, tm),
                                          lambda i: (i // tiles_per_seq, 0, i % tiles_per_seq)))
        elif kind == "out":
            out_shape.append(jax.ShapeDtypeStruct((M, width), BF16))
            out_specs.append(row_spec(width))
    riders = [_rider(w, M // tm, lambda i: i) for w in to_cast]
    return pl.pallas_call(
        kern,
        out_shape=out_shape + [r[2] for r in riders],
        grid=(M // tm,),
        in_specs=[pl.BlockSpec((tm, D), lambda i: (i, 0)),
                  pl.BlockSpec((1, N_SUB * N_MOD, D), lambda i: (i // tiles_per_seq, 0, 0)),
                  _resident(g_norm.shape),
                  _resident(w_in.shape),
                  _resident(decay.shape), _resident(xi.shape), _resident(zeta.shape),
                  _resident(g_chunk.shape)] + [r[0] for r in riders],
        out_specs=out_specs + [r[1] for r in riders],
        scratch_shapes=[pltpu.VMEM((tm, widths["rq"]), BF16),
                        pltpu.VMEM((widths["rk"], tm), BF16),
                        pltpu.VMEM((tm, widths["rv"]), BF16),
                        pltpu.VMEM((tm, widths["rg"]), BF16),
                        pltpu.VMEM((RET_HEADS, widths["rq"] // RET_HEADS, widths["rv"] // RET_HEADS), F32)],
        compiler_params=_params("arbitrary"),
        name="inproj_retention",
    )(x2d, mod3, g_norm, w_in, decay, xi, zeta, g_chunk, *to_cast)


def _retention_chunk(q_ref, kt_ref, v_ref, sg_ref, o_ref, dec_ref, xi_ref, zeta_ref, gch_ref, state_ref,
                     *, heads, dk, dv):
    C = dec_ref.shape[-1]
    q = [q_ref[:, h * dk:(h + 1) * dk] for h in range(heads)]
    kt = [kt_ref[h * dk:(h + 1) * dk, :] for h in range(heads)]
    v = [v_ref[:, h * dv:(h + 1) * dv] for h in range(heads)]

    def first_matmuls(h):
        s = jnp.dot(q[h], kt[h], preferred_element_type=F32)
        cross = jnp.dot(q[h], state_ref[h].astype(BF16), preferred_element_type=F32)
        return s, cross

    def stacked_lhs(h, s):
        intra = (s * dec_ref[h]).astype(BF16)
        kz_t = (kt[h].astype(F32) * zeta_ref[h]).astype(BF16)
        return jnp.concatenate([intra, kz_t], axis=0)

    def finish(h, both, cross):
        state_ref[h] = state_ref[h] * gch_ref[h] + both[C:, :]
        xi = jnp.concatenate([xi_ref[h]] * (dv // LANES), axis=1)
        o = both[:C, :] + cross * xi
        mu = jnp.mean(o, axis=-1, keepdims=True)
        d = o - mu
        var = jnp.mean(d * d, axis=-1, keepdims=True)
        on = d * lax.rsqrt(var + EPS)
        gate = sg_ref[:, h * dv:(h + 1) * dv].astype(F32)
        o_ref[:, h * dv:(h + 1) * dv] = (on * gate).astype(BF16)

    for h0 in range(0, heads, 2):
        pair = range(h0, min(h0 + 2, heads))
        first = [first_matmuls(h) for h in pair]
        yield
        both = [jnp.dot(stacked_lhs(h, s), v[h], preferred_element_type=F32)
                for h, (s, _) in zip(pair, first)]
        yield
        for h, b, (_, cross) in zip(pair, both, first):
            finish(h, b, cross)
            yield


def _retention_tables(heads, C):
    hh = jnp.arange(heads, dtype=F32)
    log_g = jnp.log1p(-jnp.exp2(-5.0 - hh))
    pos = jnp.arange(C, dtype=F32)
    rel = pos[:, None] - pos[None, :]
    decay = jnp.where(rel[None] >= 0, jnp.exp(jnp.maximum(rel, 0.0)[None] * log_g[:, None, None]), 0.0)
    xi = jnp.exp((pos + 1.0)[None, :] * log_g[:, None])[..., None]
    zeta = jnp.exp((C - 1.0 - pos)[None, :] * log_g[:, None])[:, None, :]
    g_chunk = jnp.exp(C * log_g)[:, None, None]
    return decay, jnp.broadcast_to(xi, (heads, C, LANES)), zeta, g_chunk


STEPS_PER_ITER = 4
DIFF_HEADS_PER_STEP = 2


def _diff_kernel(qt_ref, k_ref, vt_in_ref, lam_ref, subln_ref, slope_ref, *rest,
                 T, seq, dh, lam_init, n_step_riders, n_row_riders):
    n_riders = n_step_riders + n_row_riders
    rider_in, o_ref, rider_out = rest[:n_riders], rest[n_riders], rest[n_riders + 1:2 * n_riders + 1]
    scratch = rest[2 * n_riders + 1:]
    _run_riders(rider_in[:n_step_riders], rider_out[:n_step_riders])

    @pl.when(pl.program_id(1) == 0)
    def _():
        _run_riders(rider_in[n_step_riders:], rider_out[n_step_riders:])

    lp = lam_ref[...]
    lam = (jnp.exp(jnp.sum(lp[0:1] * lp[1:2], axis=-1, keepdims=True))
           - jnp.exp(jnp.sum(lp[2:3] * lp[3:4], axis=-1, keepdims=True)) + lam_init)
    gain = subln_ref[...] * (1.0 - lam_init)
    hw = 2 * dh
    for hh in range(DIFF_HEADS_PER_STEP):
        cols = slice(hh * hw, (hh + 1) * hw)
        _diff_head(qt_ref.at[0, cols, :], k_ref.at[0, :, cols], vt_in_ref.at[0, cols, :], slope_ref[hh],
                   lam, gain, o_ref.at[0, :, cols], *scratch, T=T, seq=seq, dh=dh)


def _diff_head(qt_ref, k_ref, vt_in_ref, slope, lam, gain, o_ref,
               vt_ref, wq_ref, kaug_ref, acc_ref, m_ref, alpha_ref, *buf_refs, T, seq, dh):
    nblk = seq // T
    n_off = nblk * (nblk - 1) // 2
    dv = 2 * dh
    dv_ext = dv + BF16_SUBLANES
    dummy = nblk

    row = lax.broadcasted_iota(jnp.int32, (T, LANES), 0)
    lane = lax.broadcasted_iota(jnp.int32, (T, LANES), 1)
    kaug_ref[...] = jnp.where(lane < 3, row, 0).astype(F32).astype(BF16)
    s_hi = slope.astype(BF16).astype(F32)
    s_mid = (slope - s_hi).astype(BF16).astype(F32)
    s_lo = slope - s_hi - s_mid
    aug_row = lax.broadcasted_iota(jnp.int32, (LANES, 2 * T), 0)
    slope_rows = jnp.where(aug_row == 0, s_hi, jnp.where(aug_row == 1, s_mid,
                           jnp.where(aug_row == 2, s_lo, 0.0))).astype(BF16)
    ones_rows = jnp.where(lax.broadcasted_iota(jnp.int32, (BF16_SUBLANES, T), 0) == 0, 1.0, 0.0).astype(BF16)
    feat_id = lax.broadcasted_iota(jnp.int32, (dv, T), 0)
    for j in range(nblk):
        cols = slice(j * T, (j + 1) * T)
        vt_ref[j, 0:dv, :] = vt_in_ref[:, cols]
        vt_ref[j, dv:dv_ext, :] = ones_rows
        qt = qt_ref[:, cols]
        zero = jnp.zeros_like(qt)
        wq_ref[j, 0:dv, :] = jnp.concatenate([jnp.where(feat_id < dh, qt, zero),
                                              jnp.where(feat_id >= dh, qt, zero)], axis=1)
        wq_ref[j, dv:dv + LANES, :] = slope_rows

    def scores(qi, j):
        k0 = j * T if isinstance(j, int) else pl.multiple_of(j * T, T)
        k_blk = k_ref[pl.ds(k0, T), :]
        return jnp.dot(jnp.concatenate([k_blk, kaug_ref[...]], axis=1), wq_ref[qi],
                       preferred_element_type=F32)

    key_id = lax.broadcasted_iota(jnp.int32, (T, 2 * T), 0)
    col_id = lax.broadcasted_iota(jnp.int32, (T, 2 * T), 1)
    causal = key_id <= jnp.where(col_id >= T, col_id - T, col_id)

    acc_ref[nblk - 1] = jnp.zeros((dv_ext, 2 * T), F32)

    U = STEPS_PER_ITER
    bufs = [buf_refs[3 * u:3 * u + 3] for u in range(U)]
    bufs[U - 1][2][...] = jnp.zeros((T, 2 * T), BF16)

    def store_scores(t_ref, tmax_ref, t, masked):
        if masked:
            t = jnp.where(causal, t, NEG)
        t_ref[...] = t
        tmax_ref[...] = jnp.max(t, axis=0, keepdims=True)

    store_scores(bufs[0][0], bufs[0][1], scores(0, 0), True)

    def next_block(qi, j):
        on_diag = j == qi
        last_diag = qi == nblk - 1
        end_of_tile = j == qi - 1
        qi_next = jnp.where(on_diag, jnp.where(last_diag, 1, qi + 1), jnp.where(end_of_tile, qi + 1, qi))
        j_next = jnp.where(on_diag, jnp.where(last_diag, 0, qi + 1), jnp.where(end_of_tile, 0, j + 1))
        return qi_next, j_next

    def step(carry, u, masked, next_masked):
        t_cur_ref, tmax_cur_ref, p_cur_ref = bufs[u]
        t_nxt_ref, tmax_nxt_ref, _ = bufs[(u + 1) % U]
        p_prv_ref = bufs[(u - 1) % U][2]
        qi, j, qi_prev, j_prev = carry
        qi_next, j_next = next_block(qi, j)

        store_scores(t_nxt_ref, tmax_nxt_ref, scores(jnp.minimum(qi_next, nblk - 1), j_next), next_masked)

        pv = jnp.dot(vt_ref[j_prev], p_prv_ref[...], preferred_element_type=F32)
        if masked:
            acc_ref[qi_prev] = pv
        else:
            acc_ref[qi_prev] = alpha_ref[...] * acc_ref[qi_prev] + pv

        t = t_cur_ref[...]
        if masked:
            m_new = tmax_cur_ref[...]
            alpha_ref[...] = jnp.zeros_like(alpha_ref)
            p_cur_ref[...] = jnp.exp2(t - m_new).astype(BF16)
        else:
            m_old = m_ref[qi]
            shift = slope * ((j - qi) * T).astype(F32)
            m_new = jnp.maximum(m_old, tmax_cur_ref[...] + shift)
            alpha_ref[...] = jnp.exp2(m_old - m_new)
            p_cur_ref[...] = jnp.exp2(t + (shift - m_new)).astype(BF16)
        m_ref[qi] = m_new
        return qi_next, j_next, qi, j

    def steps(masked, last_next_masked, _, carry):
        for u in range(U):
            carry = step(carry, u, masked, masked if u + 1 < U else last_next_masked)
        return carry

    carry = (jnp.int32(0), jnp.int32(0), jnp.int32(dummy), jnp.int32(0))
    carry = lax.fori_loop(0, nblk // STEPS_PER_ITER - 1, functools.partial(steps, True, True), carry)
    carry = steps(True, False, 0, carry)
    carry = lax.fori_loop(0, n_off // STEPS_PER_ITER, functools.partial(steps, False, False), carry)
    _, _, qi_last, j_last = carry
    acc_ref[qi_last] = (alpha_ref[...] * acc_ref[qi_last]
                        + jnp.dot(vt_ref[j_last], bufs[U - 1][2][...], preferred_element_type=F32))

    gain = jnp.broadcast_to(gain, (dv, T))
    for qi in range(nblk):
        acc = acc_ref[qi]
        r = 1.0 / acc[dv:dv + 1, :]
        out = acc[0:dv, :T] * r[:, :T] - acc[0:dv, T:] * (lam * r[:, T:])
        ms = jnp.mean(out * out, axis=0, keepdims=True)
        o_ref[qi * T:(qi + 1) * T, :] = (out * lax.rsqrt(ms + EPS) * gain).T.astype(BF16)


def _diff_attention(dqt, dk, dvt, diff_lambda, subln, *, heads, lam_init, cast_by_step=(), cast_by_row=(),
                    T=256):
    B, S, W = dk.shape
    hw = W // heads
    dh = hw // 2
    nblk = S // T
    assert T <= 256 and hw == LANES
    assert STEPS_PER_ITER % 2 == 0 and nblk % STEPS_PER_ITER == 0
    assert (nblk * (nblk - 1) // 2) % STEPS_PER_ITER == 0
    slopes = jnp.exp2(-8.0 * (jnp.arange(heads, dtype=F32) + 1.0) / heads) * LOG2E
    hps = DIFF_HEADS_PER_STEP
    groups = heads // hps
    assert groups * hps == heads
    head_spec = pl.BlockSpec((1, S, hps * hw), lambda b, g: (b, 0, g))
    head_t_spec = pl.BlockSpec((1, hps * hw, S), lambda b, g: (b, g, 0))
    riders = ([_rider(w, B * groups, lambda b, g: b * groups + g) for w in cast_by_step]
              + [_rider(w, B, lambda b, g: b) for w in cast_by_row])
    return pl.pallas_call(
        functools.partial(_diff_kernel, T=T, seq=S, dh=dh, lam_init=lam_init,
                          n_step_riders=len(cast_by_step), n_row_riders=len(cast_by_row)),
        out_shape=[jax.ShapeDtypeStruct((B, S, W), BF16)] + [r[2] for r in riders],
        grid=(B, groups),
        in_specs=[head_t_spec, head_spec, head_t_spec,
                  _resident(diff_lambda.shape),
                  _resident((hw, 1)),
                  pl.BlockSpec((hps, 1, 1), lambda b, g: (g, 0, 0))] + [r[0] for r in riders],
        out_specs=[head_spec] + [r[1] for r in riders],
        scratch_shapes=[pltpu.VMEM((nblk, hw + BF16_SUBLANES, T), BF16),
                        pltpu.VMEM((nblk, hw + LANES, 2 * T), BF16),
                        pltpu.VMEM((T, LANES), BF16),
                        pltpu.VMEM((nblk + 1, hw + BF16_SUBLANES, 2 * T), F32),
                        pltpu.VMEM((nblk, 1, 2 * T), F32),
                        pltpu.VMEM((1, 2 * T), F32)]
                       + [pltpu.VMEM((T, 2 * T), F32),
                          pltpu.VMEM((1, 2 * T), F32),
                          pltpu.VMEM((T, 2 * T), BF16)] * STEPS_PER_ITER,
        compiler_params=_params("arbitrary", "arbitrary"),
        name="diffattn",
    )(dqt, dk, dvt, diff_lambda, subln.reshape(hw, 1), slopes.reshape(heads, 1, 1),
      *cast_by_step, *cast_by_row)


def _mixout_kernel(x_ref, mod_ref, ret_ref, dif_ref, gs_ref, wro_ref, wdo_ref, wo_ref, o_ref):
    D = x_ref.shape[-1]
    y_ret = jnp.dot(ret_ref[...], wro_ref[...], preferred_element_type=F32)
    y_dif = jnp.dot(dif_ref[...], wdo_ref[...], preferred_element_type=F32)
    y = gs_ref[:, :D].astype(F32) * y_ret + gs_ref[:, D:].astype(F32) * y_dif
    gate = mod_ref[0, N_MOD * 1 + 2:N_MOD * 1 + 3, :]
    o_ref[...] = x_ref[...] + gate * jnp.dot(y.astype(BF16), wo_ref[...], preferred_element_type=F32)


def _mixout(x2d, mod3, ret2d, dif2d, gs2d, w_ro, w_do, w_o, *, seq, tm=512):
    M, D = x2d.shape
    tiles_per_seq = seq // tm
    row = lambda w: pl.BlockSpec((tm, w), lambda i: (i, 0))
    return pl.pallas_call(
        _mixout_kernel,
        out_shape=jax.ShapeDtypeStruct((M, D), F32),
        grid=(M // tm,),
        in_specs=[row(D),
                  pl.BlockSpec((1, N_SUB * N_MOD, D), lambda i: (i // tiles_per_seq, 0, 0)),
                  row(ret2d.shape[1]), row(dif2d.shape[1]), row(gs2d.shape[1]),
                  _resident(w_ro.shape), _resident(w_do.shape), _resident(w_o.shape)],
        out_specs=row(D),
        compiler_params=_params("arbitrary"),
        name="mixout",
    )(x2d, mod3, ret2d, dif2d, gs2d, w_ro, w_do, w_o)


def kernel(x, c, w_cond, b_cond, g_norm, w_ffn1_in, w_ffn1_out, w_in, w_ret_out, diff_lambda,
           diff_subln, w_diff_out, w_out, w_ffn2_in, w_ffn2_out, g_final):
    B, S, D = x.shape
    depth = w_cond.shape[0]
    diff_heads = D // (2 * DIFF_HEAD_DIM)
    xf = x.reshape(B * S, D)
    for l in range(depth):
        lam_init = 0.8 - 0.6 * math.exp(-0.3 * l)
        mod3 = _mod(c, w_cond[l], b_cond[l]).reshape(B, N_SUB * N_MOD, D)
        last = l == depth - 1

        xf, w_in_bf = _ffn(xf, mod3, g_norm[l], w_ffn1_in[l].astype(BF16), w_ffn1_out[l].astype(BF16),
                           None, sub=0, seq=S, to_cast=(w_in[l],))

        ret, dqt, dk, dvt, gs, w_ro_bf, w_do_bf, w_o_bf = _inproj(
            xf, mod3, g_norm[l], w_in_bf, batch=B, seq=S,
            to_cast=(w_ret_out[l], w_diff_out[l], w_out[l]))
        dif, w_f2i_bf, w_f2o_bf = _diff_attention(
            dqt, dk.reshape(B, S, D), dvt, diff_lambda[l], diff_subln[l], heads=diff_heads,
            lam_init=lam_init, cast_by_step=(w_ffn2_in[l],), cast_by_row=(w_ffn2_out[l],))
        xf = _mixout(xf, mod3, ret, dif.reshape(B * S, -1), gs, w_ro_bf, w_do_bf, w_o_bf, seq=S)

        (xf,) = _ffn(xf, mod3, g_norm[l], w_f2i_bf, w_f2o_bf, g_final if last else None, sub=2, seq=S)
    if depth == 0:
        raise ValueError("depth must be positive")
    return xf.reshape(B, S, D)
```

```python
import functools
import math

import jax
import jax.numpy as jnp
from jax import lax
from jax.experimental import pallas as pl
from jax.experimental.pallas import tpu as pltpu

F32 = jnp.float32
BF16 = jnp.bfloat16
EPS = 1e-6
NEG = -1e30
LOG2E = math.log2(math.e)

RET_HEADS = 4
RET_CHUNK = 256
DIFF_HEAD_DIM = 64
N_SUB = 3
N_MOD = 3

VMEM_LIMIT_BYTES = 56 * 1024 * 1024
BF16_SUBLANES = 16
LANES = 128


def _resident(shape):
    nd = len(shape)
    return pl.BlockSpec(shape, lambda *_: (0,) * nd, pipeline_mode=pl.Buffered(1))


def _params(*sem):
    return pltpu.CompilerParams(dimension_semantics=sem, vmem_limit_bytes=VMEM_LIMIT_BYTES)


def _silu(a):
    return a / (1.0 + jnp.exp(-a))


def _sigmoid(a):
    return 1.0 / (1.0 + jnp.exp(-a))


def _modulated(x, mod_ref, g_ref, sub):
    shift = mod_ref[0, N_MOD * sub:N_MOD * sub + 1, :]
    scale = mod_ref[0, N_MOD * sub + 1:N_MOD * sub + 2, :]
    ms = jnp.mean(x * x, axis=-1, keepdims=True)
    gain = g_ref[sub:sub + 1, :] * (1.0 + scale)
    return (x * lax.rsqrt(ms + EPS) * gain + shift).astype(BF16)


def _rider(w, nblocks, block_of_step):
    rows = w.shape[0] // nblocks
    assert rows * nblocks == w.shape[0] and rows % BF16_SUBLANES == 0
    spec = pl.BlockSpec((rows, w.shape[1]), lambda *g: (block_of_step(*g), 0))
    return spec, spec, jax.ShapeDtypeStruct(w.shape, BF16)


def _run_riders(in_refs, out_refs):
    for src, dst in zip(in_refs, out_refs):
        dst[...] = src[...].astype(BF16)


def _mod_kernel(c_ref, w_ref, b_ref, o_ref):
    c_act = _silu(c_ref[...]).astype(BF16)
    o_ref[...] = jnp.dot(c_act, w_ref[...].astype(BF16), preferred_element_type=F32) + b_ref[...]


def _mod(c, w, b):
    B, D = c.shape
    N = w.shape[1]
    tn = 2304
    assert N % tn == 0
    return pl.pallas_call(
        _mod_kernel,
        out_shape=jax.ShapeDtypeStruct((B, N), F32),
        grid=(N // tn,),
        in_specs=[pl.BlockSpec((B, D), lambda j: (0, 0)),
                  pl.BlockSpec((D, tn), lambda j: (0, j)),
                  pl.BlockSpec((1, tn), lambda j: (0, j))],
        out_specs=pl.BlockSpec((B, tn), lambda j: (0, j)),
        compiler_params=_params("arbitrary"),
        name="mod",
    )(c, w, b.reshape(1, N))


def _mixer_merge(x, mod_ref, ret_ref, dif_ref, gs_ref, wro_ref, wdo_ref, wo_ref):
    D = x.shape[-1]
    y_ret = jnp.dot(ret_ref[...], wro_ref[...], preferred_element_type=F32)
    y_dif = jnp.dot(dif_ref[...], wdo_ref[...], preferred_element_type=F32)
    y = gs_ref[:, :D].astype(F32) * y_ret + gs_ref[:, D:].astype(F32) * y_dif
    gate = mod_ref[0, N_MOD * 1 + 2:N_MOD * 1 + 3, :]
    return x + gate * jnp.dot(y.astype(BF16), wo_ref[...], preferred_element_type=F32)


def _ffn_kernel(x_ref, mod_ref, g_ref, win_ref, wout_ref, *rest, sub, d_ff, tf, final, mix, n_riders):
    rest = list(rest)
    gfin_ref = rest.pop(0) if final else None
    mix_refs = [rest.pop(0) for _ in range(6)] if mix else None
    rider_in = [rest.pop(0) for _ in range(n_riders)]
    o_ref = rest.pop(0)
    rider_out = [rest.pop(0) for _ in range(n_riders)]
    (act_ref,) = rest
    _run_riders(rider_in, rider_out)
    x = x_ref[...]
    if mix:
        x = _mixer_merge(x, mod_ref, *mix_refs)
    h = _modulated(x, mod_ref, g_ref, sub)
    for c in range(d_ff // tf):
        a = jnp.dot(h, win_ref[:, c * tf:(c + 1) * tf], preferred_element_type=F32)
        b = jnp.dot(h, win_ref[:, d_ff + c * tf:d_ff + (c + 1) * tf], preferred_element_type=F32)
        act_ref[:, c * tf:(c + 1) * tf] = (_silu(a) * b).astype(BF16)
    y = jnp.dot(act_ref[...], wout_ref[...], preferred_element_type=F32)
    gate = mod_ref[0, N_MOD * sub + 2:N_MOD * sub + 3, :]
    out = x + (0.5 * gate) * y
    if final:
        ms = jnp.mean(out * out, axis=-1, keepdims=True)
        out = out * lax.rsqrt(ms + EPS) * gfin_ref[...]
    o_ref[...] = out


def _ffn(x2d, mod3, g_norm, w_in, w_out, g_final, *, sub, seq, mix=None, to_cast=(), tm=512, tf=256):
    M, D = x2d.shape
    d_ff = w_out.shape[0]
    tiles_per_seq = seq // tm
    final = g_final is not None
    row = lambda width: pl.BlockSpec((tm, width), lambda i: (i, 0))
    in_specs = [row(D),
                pl.BlockSpec((1, N_SUB * N_MOD, D), lambda i: (i // tiles_per_seq, 0, 0)),
                _resident(g_norm.shape),
                _resident(w_in.shape),
                _resident(w_out.shape)]
    args = [x2d, mod3, g_norm, w_in, w_out]
    if final:
        in_specs.append(_resident((1, D)))
        args.append(g_final.reshape(1, D))
    if mix is not None:
        in_specs += [row(a.shape[1]) for a in mix[:3]] + [_resident(w.shape) for w in mix[3:]]
        args += list(mix)
    riders = [_rider(w, M // tm, lambda i: i) for w in to_cast]
    return pl.pallas_call(
        functools.partial(_ffn_kernel, sub=sub, d_ff=d_ff, tf=tf, final=final, mix=mix is not None,
                          n_riders=len(riders)),
        out_shape=[jax.ShapeDtypeStruct((M, D), F32)] + [r[2] for r in riders],
        grid=(M // tm,),
        in_specs=in_specs + [r[0] for r in riders],
        out_specs=[row(D)] + [r[1] for r in riders],
        scratch_shapes=[pltpu.VMEM((tm, d_ff), BF16)],
        compiler_params=_params("arbitrary"),
        name="ffn%d" % sub,
    )(*args, *to_cast)


def _inproj_segments(D):
    return (("rq", D, None, "ret"), ("rk", D, "ret_k_scale", "ret_t"), ("rv", 2 * D, None, "ret"),
            ("rg", 2 * D, "silu", "ret"), ("dq", D, "diff_q_scale", "out_t"), ("dk", D, None, "out"),
            ("dv", D, None, "out_t"), ("gates", 2 * D, "sigmoid", "out"))


def _inproj_kernel(x_ref, mod_ref, g_ref, w_ref, dec_ref, xi_ref, zeta_ref, gch_ref, *rest,
                   segs, tn, ret_k_scale, diff_q_scale, tiles_per_seq, ret_heads, n_riders):
    rest = list(rest)
    rider_in = [rest.pop(0) for _ in range(n_riders)]
    ret_ref = rest.pop(0)
    dest = {s[0]: rest.pop(0) for s in segs if s[3].startswith("out")}
    rider_out = [rest.pop(0) for _ in range(n_riders)]
    dest["rq"], dest["rk"], dest["rv"], dest["rg"], state_ref = rest
    _run_riders(rider_in, rider_out)

    @pl.when(lax.rem(pl.program_id(0), tiles_per_seq) == 0)
    def _():
        state_ref[...] = jnp.zeros_like(state_ref)

    h = _modulated(x_ref[...], mod_ref, g_ref, 1)
    first_col, col = {}, 0
    for name, width, _, _ in segs:
        first_col[name] = col
        col += width
    ret_segs = [s for s in segs if s[3].startswith("ret")]
    out_segs = sorted((s for s in segs if not s[3].startswith("ret")),
                      key=lambda s: (s[2] is None, s[3] == "out"))
    retention_stages = iter(())
    for name, width, epi, kind in ret_segs + out_segs:
        o_ref = dest[name]
        col = first_col[name]
        for c in range(width // tn):
            next(retention_stages, None)
            y = jnp.dot(h, w_ref[:, col + c * tn:col + (c + 1) * tn], preferred_element_type=F32)
            if epi == "silu":
                y = _silu(y)
            elif epi == "sigmoid":
                y = _sigmoid(y)
            elif epi == "ret_k_scale":
                y = y * ret_k_scale
            elif epi == "diff_q_scale":
                y = y * diff_q_scale
            if kind == "out_t":
                o_ref[0, c * tn:(c + 1) * tn, :] = y.astype(BF16).T
            elif kind == "ret_t":
                o_ref[c * tn:(c + 1) * tn, :] = y.astype(BF16).T
            else:
                o_ref[:, c * tn:(c + 1) * tn] = y.astype(BF16)
        if name == ret_segs[-1][0]:
            qk_w, v_w = dest["rq"].shape[1], dest["rv"].shape[1]
            retention_stages = _retention_chunk(
                dest["rq"], dest["rk"], dest["rv"], dest["rg"], ret_ref,
                dec_ref, xi_ref, zeta_ref, gch_ref, state_ref,
                heads=ret_heads, dk=qk_w // ret_heads, dv=v_w // ret_heads)
    for _ in retention_stages:
        pass


def _inproj(x2d, mod3, g_norm, w_in, *, batch, seq, to_cast=(), tn=512):
    M, D = x2d.shape
    tm = RET_CHUNK
    segs = _inproj_segments(D)
    assert sum(s[1] for s in segs) == w_in.shape[1]
    tiles_per_seq = seq // tm
    widths = {s[0]: s[1] for s in segs}
    decay, xi, zeta, g_chunk = _retention_tables(RET_HEADS, tm)
    kern = functools.partial(
        _inproj_kernel, segs=segs, tn=tn, tiles_per_seq=tiles_per_seq, ret_heads=RET_HEADS,
        ret_k_scale=float((D // RET_HEADS) ** -0.5),
        diff_q_scale=LOG2E * DIFF_HEAD_DIM ** -0.5,
        n_riders=len(to_cast))
    row_spec = lambda width: pl.BlockSpec((tm, width), lambda i: (i, 0))
    out_shape, out_specs = [jax.ShapeDtypeStruct((M, widths["rv"]), BF16)], [row_spec(widths["rv"])]
    for _, width, _, kind in segs:
        if kind == "out_t":
            out_shape.append(jax.ShapeDtypeStruct((batch, width, seq), BF16))
            out_specs.append(pl.BlockSpec((1, width, tm),
                                          lambda i: (i // tiles_per_seq, 0, i % tiles_per_seq)))
        elif kind == "out":
            out_shape.append(jax.ShapeDtypeStruct((M, width), BF16))
            out_specs.append(row_spec(width))
    riders = [_rider(w, M // tm, lambda i: i) for w in to_cast]
    return pl.pallas_call(
        kern,
        out_shape=out_shape + [r[2] for r in riders],
        grid=(M // tm,),
        in_specs=[pl.BlockSpec((tm, D), lambda i: (i, 0)),
                  pl.BlockSpec((1, N_SUB * N_MOD, D), lambda i: (i // tiles_per_seq, 0, 0)),
                  _resident(g_norm.shape),
                  _resident(w_in.shape),
                  _resident(decay.shape), _resident(xi.shape), _resident(zeta.shape),
                  _resident(g_chunk.shape)] + [r[0] for r in riders],
        out_specs=out_specs + [r[1] for r in riders],
        scratch_shapes=[pltpu.VMEM((tm, widths["rq"]), BF16),
                        pltpu.VMEM((widths["rk"], tm), BF16),
                        pltpu.VMEM((tm, widths["rv"]), BF16),
                        pltpu.VMEM((tm, widths["rg"]), BF16),
                        pltpu.VMEM((RET_HEADS, widths["rq"] // RET_HEADS, widths["rv"] // RET_HEADS), F32)],
        compiler_params=_params("arbitrary"),
        name="inproj_retention",
    )(x2d, mod3, g_norm, w_in, decay, xi, zeta, g_chunk, *to_cast)


def _retention_chunk(q_ref, kt_ref, v_ref, sg_ref, o_ref, dec_ref, xi_ref, zeta_ref, gch_ref, state_ref,
                     *, heads, dk, dv):
    C = dec_ref.shape[-1]
    q = [q_ref[:, h * dk:(h + 1) * dk] for h in range(heads)]
    kt = [kt_ref[h * dk:(h + 1) * dk, :] for h in range(heads)]
    v = [v_ref[:, h * dv:(h + 1) * dv] for h in range(heads)]

    def first_matmuls(h):
        s = jnp.dot(q[h], kt[h], preferred_element_type=F32)
        cross = jnp.dot(q[h], state_ref[h].astype(BF16), preferred_element_type=F32)
        return s, cross

    def stacked_lhs(h, s):
        intra = (s * dec_ref[h]).astype(BF16)
        kz_t = (kt[h].astype(F32) * zeta_ref[h]).astype(BF16)
        return jnp.concatenate([intra, kz_t], axis=0)

    def finish(h, both, cross):
        state_ref[h] = state_ref[h] * gch_ref[h] + both[C:, :]
        xi = jnp.concatenate([xi_ref[h]] * (dv // LANES), axis=1)
        o = both[:C, :] + cross * xi
        mu = jnp.mean(o, axis=-1, keepdims=True)
        d = o - mu
        var = jnp.mean(d * d, axis=-1, keepdims=True)
        on = d * lax.rsqrt(var + EPS)
        gate = sg_ref[:, h * dv:(h + 1) * dv].astype(F32)
        o_ref[:, h * dv:(h + 1) * dv] = (on * gate).astype(BF16)

    for h0 in range(0, heads, 2):
        pair = range(h0, min(h0 + 2, heads))
        first = [first_matmuls(h) for h in pair]
        yield
        both = [jnp.dot(stacked_lhs(h, s), v[h], preferred_element_type=F32)
                for h, (s, _) in zip(pair, first)]
        yield
        for h, b, (_, cross) in zip(pair, both, first):
            finish(h, b, cross)
            yield


def _retention_tables(heads, C):
    hh = jnp.arange(heads, dtype=F32)
    log_g = jnp.log1p(-jnp.exp2(-5.0 - hh))
    pos = jnp.arange(C, dtype=F32)
    rel = pos[:, None] - pos[None, :]
    decay = jnp.where(rel[None] >= 0, jnp.exp(jnp.maximum(rel, 0.0)[None] * log_g[:, None, None]), 0.0)
    xi = jnp.exp((pos + 1.0)[None, :] * log_g[:, None])[..., None]
    zeta = jnp.exp((C - 1.0 - pos)[None, :] * log_g[:, None])[:, None, :]
    g_chunk = jnp.exp(C * log_g)[:, None, None]
    return decay, jnp.broadcast_to(xi, (heads, C, LANES)), zeta, g_chunk


STEPS_PER_ITER = 4
DIFF_HEADS_PER_STEP = 2


def _diff_kernel(qt_ref, k_ref, vt_in_ref, lam_ref, subln_ref, slope_ref, *rest,
                 T, seq, dh, lam_init, n_step_riders, n_row_riders):
    n_riders = n_step_riders + n_row_riders
    rider_in, o_ref, rider_out = rest[:n_riders], rest[n_riders], rest[n_riders + 1:2 * n_riders + 1]
    scratch = rest[2 * n_riders + 1:]
    _run_riders(rider_in[:n_step_riders], rider_out[:n_step_riders])

    @pl.when(pl.program_id(1) == 0)
    def _():
        _run_riders(rider_in[n_step_riders:], rider_out[n_step_riders:])

    lp = lam_ref[...]
    lam = (jnp.exp(jnp.sum(lp[0:1] * lp[1:2], axis=-1, keepdims=True))
           - jnp.exp(jnp.sum(lp[2:3] * lp[3:4], axis=-1, keepdims=True)) + lam_init)
    gain = subln_ref[...] * (1.0 - lam_init)
    hw = 2 * dh
    for hh in range(DIFF_HEADS_PER_STEP):
        cols = slice(hh * hw, (hh + 1) * hw)
        _diff_head(qt_ref.at[0, cols, :], k_ref.at[0, :, cols], vt_in_ref.at[0, cols, :], slope_ref[hh],
                   lam, gain, o_ref.at[0, :, cols], *scratch, T=T, seq=seq, dh=dh)


def _diff_head(qt_ref, k_ref, vt_in_ref, slope, lam, gain, o_ref,
               vt_ref, wq_ref, kaug_ref, acc_ref, m_ref, alpha_ref, *buf_refs, T, seq, dh):
    nblk = seq // T
    n_off = nblk * (nblk - 1) // 2
    dv = 2 * dh
    dv_ext = dv + BF16_SUBLANES
    dummy = nblk

    row = lax.broadcasted_iota(jnp.int32, (T, LANES), 0)
    lane = lax.broadcasted_iota(jnp.int32, (T, LANES), 1)
    kaug_ref[...] = jnp.where(lane < 3, row, 0).astype(F32).astype(BF16)
    s_hi = slope.astype(BF16).astype(F32)
    s_mid = (slope - s_hi).astype(BF16).astype(F32)
    s_lo = slope - s_hi - s_mid
    aug_row = lax.broadcasted_iota(jnp.int32, (LANES, 2 * T), 0)
    slope_rows = jnp.where(aug_row == 0, s_hi, jnp.where(aug_row == 1, s_mid,
                           jnp.where(aug_row == 2, s_lo, 0.0))).astype(BF16)
    ones_rows = jnp.where(lax.broadcasted_iota(jnp.int32, (BF16_SUBLANES, T), 0) == 0, 1.0, 0.0).astype(BF16)
    feat_id = lax.broadcasted_iota(jnp.int32, (dv, T), 0)
    for j in range(nblk):
        cols = slice(j * T, (j + 1) * T)
        vt_ref[j, 0:dv, :] = vt_in_ref[:, cols]
        vt_ref[j, dv:dv_ext, :] = ones_rows
        qt = qt_ref[:, cols]
        zero = jnp.zeros_like(qt)
        wq_ref[j, 0:dv, :] = jnp.concatenate([jnp.where(feat_id < dh, qt, zero),
                                              jnp.where(feat_id >= dh, qt, zero)], axis=1)
        wq_ref[j, dv:dv + LANES, :] = slope_rows

    def scores(qi, j):
        k0 = j * T if isinstance(j, int) else pl.multiple_of(j * T, T)
        k_blk = k_ref[pl.ds(k0, T), :]
        return jnp.dot(jnp.concatenate([k_blk, kaug_ref[...]], axis=1), wq_ref[qi],
                       preferred_element_type=F32)

    key_id = lax.broadcasted_iota(jnp.int32, (T, 2 * T), 0)
    col_id = lax.broadcasted_iota(jnp.int32, (T, 2 * T), 1)
    causal = key_id <= jnp.where(col_id >= T, col_id - T, col_id)

    acc_ref[nblk - 1] = jnp.zeros((dv_ext, 2 * T), F32)

    U = STEPS_PER_ITER
    bufs = [buf_refs[3 * u:3 * u + 3] for u in range(U)]
    bufs[U - 1][2][...] = jnp.zeros((T, 2 * T), BF16)

    def store_scores(t_ref, tmax_ref, t, masked):
        if masked:
            t = jnp.where(causal, t, NEG)
        t_ref[...] = t
        tmax_ref[...] = jnp.max(t, axis=0, keepdims=True)

    store_scores(bufs[0][0], bufs[0][1], scores(0, 0), True)

    def next_block(qi, j):
        on_diag = j == qi
        last_diag = qi == nblk - 1
        end_of_tile = j == qi - 1
        qi_next = jnp.where(on_diag, jnp.where(last_diag, 1, qi + 1), jnp.where(end_of_tile, qi + 1, qi))
        j_next = jnp.where(on_diag, jnp.where(last_diag, 0, qi + 1), jnp.where(end_of_tile, 0, j + 1))
        return qi_next, j_next

    def step(carry, u, masked, next_masked):
        t_cur_ref, tmax_cur_ref, p_cur_ref = bufs[u]
        t_nxt_ref, tmax_nxt_ref, _ = bufs[(u + 1) % U]
        p_prv_ref = bufs[(u - 1) % U][2]
        qi, j, qi_prev, j_prev = carry
        qi_next, j_next = next_block(qi, j)

        store_scores(t_nxt_ref, tmax_nxt_ref, scores(jnp.minimum(qi_next, nblk - 1), j_next), next_masked)

        pv = jnp.dot(vt_ref[j_prev], p_prv_ref[...], preferred_element_type=F32)
        if masked:
            acc_ref[qi_prev] = pv
        else:
            acc_ref[qi_prev] = alpha_ref[...] * acc_ref[qi_prev] + pv

        t = t_cur_ref[...]
        if masked:
            m_new = tmax_cur_ref[...]
            alpha_ref[...] = jnp.zeros_like(alpha_ref)
            p_cur_ref[...] = jnp.exp2(t - m_new).astype(BF16)
        else:
            m_old = m_ref[qi]
            shift = slope * ((j - qi) * T).astype(F32)
            m_new = jnp.maximum(m_old, tmax_cur_ref[...] + shift)
            alpha_ref[...] = jnp.exp2(m_old - m_new)
            p_cur_ref[...] = jnp.exp2(t + (shift - m_new)).astype(BF16)
        m_ref[qi] = m_new
        return qi_next, j_next, qi, j

    def steps(masked, last_next_masked, _, carry):
        for u in range(U):
            carry = step(carry, u, masked, masked if u + 1 < U else last_next_masked)
        return carry

    carry = (jnp.int32(0), jnp.int32(0), jnp.int32(dummy), jnp.int32(0))
    carry = lax.fori_loop(0, nblk // STEPS_PER_ITER - 1, functools.partial(steps, True, True), carry)
    carry = steps(True, False, 0, carry)
    carry = lax.fori_loop(0, n_off // STEPS_PER_ITER, functools.partial(steps, False, False), carry)
    _, _, qi_last, j_last = carry
    acc_ref[qi_last] = (alpha_ref[...] * acc_ref[qi_last]
                        + jnp.dot(vt_ref[j_last], bufs[U - 1][2][...], preferred_element_type=F32))

    gain = jnp.broadcast_to(gain, (dv, T))
    for qi in range(nblk):
        acc = acc_ref[qi]
        r = 1.0 / acc[dv:dv + 1, :]
        out = acc[0:dv, :T] * r[:, :T] - acc[0:dv, T:] * (lam * r[:, T:])
        ms = jnp.mean(out * out, axis=0, keepdims=True)
        o_ref[qi * T:(qi + 1) * T, :] = (out * lax.rsqrt(ms + EPS) * gain).T.astype(BF16)


def _diff_attention(dqt, dk, dvt, diff_lambda, subln, *, heads, lam_init, cast_by_step=(), cast_by_row=(),
                    T=256):
    B, S, W = dk.shape
    hw = W // heads
    dh = hw // 2
    nblk = S // T
    assert T <= 256 and hw == LANES
    assert STEPS_PER_ITER % 2 == 0 and nblk % STEPS_PER_ITER == 0
    assert (nblk * (nblk - 1) // 2) % STEPS_PER_ITER == 0
    slopes = jnp.exp2(-8.0 * (jnp.arange(heads, dtype=F32) + 1.0) / heads) * LOG2E
    hps = DIFF_HEADS_PER_STEP
    groups = heads // hps
    assert groups * hps == heads
    head_spec = pl.BlockSpec((1, S, hps * hw), lambda b, g: (b, 0, g))
    head_t_spec = pl.BlockSpec((1, hps * hw, S), lambda b, g: (b, g, 0))
    riders = ([_rider(w, B * groups, lambda b, g: b * groups + g) for w in cast_by_step]
              + [_rider(w, B, lambda b, g: b) for w in cast_by_row])
    return pl.pallas_call(
        functools.partial(_diff_kernel, T=T, seq=S, dh=dh, lam_init=lam_init,
                          n_step_riders=len(cast_by_step), n_row_riders=len(cast_by_row)),
        out_shape=[jax.ShapeDtypeStruct((B, S, W), BF16)] + [r[2] for r in riders],
        grid=(B, groups),
        in_specs=[head_t_spec, head_spec, head_t_spec,
                  _resident(diff_lambda.shape),
                  _resident((hw, 1)),
                  pl.BlockSpec((hps, 1, 1), lambda b, g: (g, 0, 0))] + [r[0] for r in riders],
        out_specs=[head_spec] + [r[1] for r in riders],
        scratch_shapes=[pltpu.VMEM((nblk, hw + BF16_SUBLANES, T), BF16),
                        pltpu.VMEM((nblk, hw + LANES, 2 * T), BF16),
                        pltpu.VMEM((T, LANES), BF16),
                        pltpu.VMEM((nblk + 1, hw + BF16_SUBLANES, 2 * T), F32),
                        pltpu.VMEM((nblk, 1, 2 * T), F32),
                        pltpu.VMEM((1, 2 * T), F32)]
                       + [pltpu.VMEM((T, 2 * T), F32),
                          pltpu.VMEM((1, 2 * T), F32),
                          pltpu.VMEM((T, 2 * T), BF16)] * STEPS_PER_ITER,
        compiler_params=_params("arbitrary", "arbitrary"),
        name="diffattn",
    )(dqt, dk, dvt, diff_lambda, subln.reshape(hw, 1), slopes.reshape(heads, 1, 1),
      *cast_by_step, *cast_by_row)


def kernel(x, c, w_cond, b_cond, g_norm, w_ffn1_in, w_ffn1_out, w_in, w_ret_out, diff_lambda,
           diff_subln, w_diff_out, w_out, w_ffn2_in, w_ffn2_out, g_final):
    B, S, D = x.shape
    depth = w_cond.shape[0]
    diff_heads = D // (2 * DIFF_HEAD_DIM)
    xf = x.reshape(B * S, D)
    for l in range(depth):
        lam_init = 0.8 - 0.6 * math.exp(-0.3 * l)
        mod3 = _mod(c, w_cond[l], b_cond[l]).reshape(B, N_SUB * N_MOD, D)
        last = l == depth - 1

        xf, w_in_bf = _ffn(xf, mod3, g_norm[l], w_ffn1_in[l].astype(BF16), w_ffn1_out[l].astype(BF16),
                           None, sub=0, seq=S, to_cast=(w_in[l],))

        ret, dqt, dk, dvt, gs, w_ro_bf, w_do_bf, w_o_bf = _inproj(
            xf, mod3, g_norm[l], w_in_bf, batch=B, seq=S,
            to_cast=(w_ret_out[l], w_diff_out[l], w_out[l]))
        dif, w_f2i_bf, w_f2o_bf = _diff_attention(
            dqt, dk.reshape(B, S, D), dvt, diff_lambda[l], diff_subln[l], heads=diff_heads,
            lam_init=lam_init, cast_by_step=(w_ffn2_in[l],), cast_by_row=(w_ffn2_out[l],))
        (xf,) = _ffn(xf, mod3, g_norm[l], w_f2i_bf, w_f2o_bf, g_final if last else None, sub=2, seq=S,
                     mix=(ret, dif.reshape(B * S, -1), gs, w_ro_bf, w_do_bf, w_o_bf))
    if depth == 0:
        raise ValueError("depth must be positive")
    return xf.reshape(B, S, D)
```

```python
import functools
import math

import jax
import jax.numpy as jnp
from jax import lax
from jax.experimental import pallas as pl
from jax.experimental.pallas import tpu as pltpu

F32 = jnp.float32
BF16 = jnp.bfloat16
EPS = 1e-6
NEG = -1e30
LOG2E = math.log2(math.e)

RET_HEADS = 4
RET_CHUNK = 256
DIFF_HEAD_DIM = 64
N_SUB = 3
N_MOD = 3

VMEM_LIMIT_BYTES = 56 * 1024 * 1024
BF16_SUBLANES = 16
LANES = 128


def _resident(shape):
    nd = len(shape)
    return pl.BlockSpec(shape, lambda *_: (0,) * nd, pipeline_mode=pl.Buffered(1))


def _params(*sem):
    return pltpu.CompilerParams(dimension_semantics=sem, vmem_limit_bytes=VMEM_LIMIT_BYTES)


def _silu(a):
    return a / (1.0 + jnp.exp(-a))


def _sigmoid(a):
    return 1.0 / (1.0 + jnp.exp(-a))


def _modulated(x, mod_ref, g_ref, sub):
    shift = mod_ref[0, N_MOD * sub:N_MOD * sub + 1, :]
    scale = mod_ref[0, N_MOD * sub + 1:N_MOD * sub + 2, :]
    ms = jnp.mean(x * x, axis=-1, keepdims=True)
    gain = g_ref[sub:sub + 1, :] * (1.0 + scale)
    return (x * lax.rsqrt(ms + EPS) * gain + shift).astype(BF16)


def _rider(w, nblocks, block_of_step):
    rows = w.shape[0] // nblocks
    assert rows * nblocks == w.shape[0] and rows % BF16_SUBLANES == 0
    spec = pl.BlockSpec((rows, w.shape[1]), lambda *g: (block_of_step(*g), 0))
    return spec, spec, jax.ShapeDtypeStruct(w.shape, BF16)


def _run_riders(in_refs, out_refs):
    for src, dst in zip(in_refs, out_refs):
        dst[...] = src[...].astype(BF16)


def _mod_kernel(c_ref, w_ref, b_ref, o_ref):
    c_act = _silu(c_ref[...]).astype(BF16)
    o_ref[...] = jnp.dot(c_act, w_ref[...].astype(BF16), preferred_element_type=F32) + b_ref[...]


def _mod(c, w, b):
    B, D = c.shape
    N = w.shape[1]
    tn = 2304
    assert N % tn == 0
    return pl.pallas_call(
        _mod_kernel,
        out_shape=jax.ShapeDtypeStruct((B, N), F32),
        grid=(N // tn,),
        in_specs=[pl.BlockSpec((B, D), lambda j: (0, 0)),
                  pl.BlockSpec((D, tn), lambda j: (0, j)),
                  pl.BlockSpec((1, tn), lambda j: (0, j))],
        out_specs=pl.BlockSpec((B, tn), lambda j: (0, j)),
        compiler_params=_params("arbitrary"),
        name="mod",
    )(c, w, b.reshape(1, N))


def _mixer_merge(x, mod_ref, ret_ref, dif_ref, gs_ref, wro_ref, wdo_ref, wo_ref):
    D = x.shape[-1]
    y_ret = jnp.dot(ret_ref[...], wro_ref[...], preferred_element_type=F32)
    y_dif = jnp.dot(dif_ref[...], wdo_ref[...], preferred_element_type=F32)
    y = gs_ref[:, :D].astype(F32) * y_ret + gs_ref[:, D:].astype(F32) * y_dif
    gate = mod_ref[0, N_MOD * 1 + 2:N_MOD * 1 + 3, :]
    return x + gate * jnp.dot(y.astype(BF16), wo_ref[...], preferred_element_type=F32)


def _ffn_kernel(x_ref, mod_ref, g_ref, win_ref, wout_ref, *rest, sub, d_ff, tf, final, mix, n_riders):
    rest = list(rest)
    gfin_ref = rest.pop(0) if final else None
    mix_refs = [rest.pop(0) for _ in range(6)] if mix else None
    rider_in = [rest.pop(0) for _ in range(n_riders)]
    o_ref = rest.pop(0)
    rider_out = [rest.pop(0) for _ in range(n_riders)]
    (act_ref,) = rest
    _run_riders(rider_in, rider_out)
    x = x_ref[...]
    if mix:
        x = _mixer_merge(x, mod_ref, *mix_refs)
    h = _modulated(x, mod_ref, g_ref, sub)
    for c in range(d_ff // tf):
        a = jnp.dot(h, win_ref[:, c * tf:(c + 1) * tf], preferred_element_type=F32)
        b = jnp.dot(h, win_ref[:, d_ff + c * tf:d_ff + (c + 1) * tf], preferred_element_type=F32)
        act_ref[:, c * tf:(c + 1) * tf] = (_silu(a) * b).astype(BF16)
    y = jnp.dot(act_ref[...], wout_ref[...], preferred_element_type=F32)
    gate = mod_ref[0, N_MOD * sub + 2:N_MOD * sub + 3, :]
    out = x + (0.5 * gate) * y
    if final:
        ms = jnp.mean(out * out, axis=-1, keepdims=True)
        out = out * lax.rsqrt(ms + EPS) * gfin_ref[...]
    o_ref[...] = out


def _ffn(x2d, mod3, g_norm, w_in, w_out, g_final, *, sub, seq, mix=None, to_cast=(), tm=512, tf=256):
    M, D = x2d.shape
    d_ff = w_out.shape[0]
    tiles_per_seq = seq // tm
    final = g_final is not None
    row = lambda width: pl.BlockSpec((tm, width), lambda i: (i, 0))
    in_specs = [row(D),
                pl.BlockSpec((1, N_SUB * N_MOD, D), lambda i: (i // tiles_per_seq, 0, 0)),
                _resident(g_norm.shape),
                _resident(w_in.shape),
                _resident(w_out.shape)]
    args = [x2d, mod3, g_norm, w_in, w_out]
    if final:
        in_specs.append(_resident((1, D)))
        args.append(g_final.reshape(1, D))
    if mix is not None:
        in_specs += [row(a.shape[1]) for a in mix[:3]] + [_resident(w.shape) for w in mix[3:]]
        args += list(mix)
    riders = [_rider(w, M // tm, lambda i: i) for w in to_cast]
    return pl.pallas_call(
        functools.partial(_ffn_kernel, sub=sub, d_ff=d_ff, tf=tf, final=final, mix=mix is not None,
                          n_riders=len(riders)),
        out_shape=[jax.ShapeDtypeStruct((M, D), F32)] + [r[2] for r in riders],
        grid=(M // tm,),
        in_specs=in_specs + [r[0] for r in riders],
        out_specs=[row(D)] + [r[1] for r in riders],
        scratch_shapes=[pltpu.VMEM((tm, d_ff), BF16)],
        compiler_params=_params("arbitrary"),
        name="ffn%d" % sub,
    )(*args, *to_cast)


def _inproj_segments(D):
    return (("rq", D, None, "ret"), ("rk", D, "ret_k_scale", "ret_t"), ("rv", 2 * D, None, "ret"),
            ("rg", 2 * D, "silu", "ret"), ("dq", D, "diff_q_scale", "out_t"), ("dk", D, None, "out"),
            ("dv", D, None, "out_t"), ("gates", 2 * D, "sigmoid", "out"))


def _inproj_kernel(x_ref, mod_ref, g_ref, w_ref, dec_ref, xi_ref, zeta_ref, gch_ref, *rest,
                   segs, tn, ret_k_scale, diff_q_scale, tiles_per_seq, ret_heads, n_riders):
    rest = list(rest)
    rider_in = [rest.pop(0) for _ in range(n_riders)]
    ret_ref = rest.pop(0)
    dest = {s[0]: rest.pop(0) for s in segs if s[3].startswith("out")}
    rider_out = [rest.pop(0) for _ in range(n_riders)]
    dest["rq"], dest["rk"], dest["rv"], dest["rg"], state_ref = rest
    _run_riders(rider_in, rider_out)

    @pl.when(lax.rem(pl.program_id(0), tiles_per_seq) == 0)
    def _():
        state_ref[...] = jnp.zeros_like(state_ref)

    h = _modulated(x_ref[...], mod_ref, g_ref, 1)
    first_col, col = {}, 0
    for name, width, _, _ in segs:
        first_col[name] = col
        col += width
    ret_segs = [s for s in segs if s[3].startswith("ret")]
    out_segs = sorted((s for s in segs if not s[3].startswith("ret")),
                      key=lambda s: (s[2] is None, s[3] == "out"))
    retention_stages = iter(())
    for name, width, epi, kind in ret_segs + out_segs:
        o_ref = dest[name]
        col = first_col[name]
        for c in range(width // tn):
            next(retention_stages, None)
            y = jnp.dot(h, w_ref[:, col + c * tn:col + (c + 1) * tn], preferred_element_type=F32)
            if epi == "silu":
                y = _silu(y)
            elif epi == "sigmoid":
                y = _sigmoid(y)
            elif epi == "ret_k_scale":
                y = y * ret_k_scale
            elif epi == "diff_q_scale":
                y = y * diff_q_scale
            if kind == "out_t":
                o_ref[0, c * tn:(c + 1) * tn, :] = y.astype(BF16).T
            elif kind == "ret_t":
                o_ref[c * tn:(c + 1) * tn, :] = y.astype(BF16).T
            else:
                o_ref[:, c * tn:(c + 1) * tn] = y.astype(BF16)
        if name == ret_segs[-1][0]:
            qk_w, v_w = dest["rq"].shape[1], dest["rv"].shape[1]
            retention_stages = _retention_chunk(
                dest["rq"], dest["rk"], dest["rv"], dest["rg"], ret_ref,
                dec_ref, xi_ref, zeta_ref, gch_ref, state_ref,
                heads=ret_heads, dk=qk_w // ret_heads, dv=v_w // ret_heads)
    for _ in retention_stages:
        pass


def _inproj(x2d, mod3, g_norm, w_in, *, batch, seq, to_cast=(), tn=512):
    M, D = x2d.shape
    tm = RET_CHUNK
    segs = _inproj_segments(D)
    assert sum(s[1] for s in segs) == w_in.shape[1]
    tiles_per_seq = seq // tm
    widths = {s[0]: s[1] for s in segs}
    decay, xi, zeta, g_chunk = _retention_tables(RET_HEADS, tm)
    kern = functools.partial(
        _inproj_kernel, segs=segs, tn=tn, tiles_per_seq=tiles_per_seq, ret_heads=RET_HEADS,
        ret_k_scale=float((D // RET_HEADS) ** -0.5),
        diff_q_scale=LOG2E * DIFF_HEAD_DIM ** -0.5,
        n_riders=len(to_cast))
    row_spec = lambda width: pl.BlockSpec((tm, width), lambda i: (i, 0))
    out_shape, out_specs = [jax.ShapeDtypeStruct((M, widths["rv"]), BF16)], [row_spec(widths["rv"])]
    for _, width, _, kind in segs:
        if kind == "out_t":
            out_shape.append(jax.ShapeDtypeStruct((batch, width, seq), BF16))
            out_specs.append(pl.BlockSpec((1, width, tm),
                                          lambda i: (i // tiles_per_seq, 0, i % tiles_per_seq)))
        elif kind == "out":
            out_shape.append(jax.ShapeDtypeStruct((M, width), BF16))
            out_specs.append(row_spec(width))
    riders = [_rider(w, M // tm, lambda i: i) for w in to_cast]
    return pl.pallas_call(
        kern,
        out_shape=out_shape + [r[2] for r in riders],
        grid=(M // tm,),
        in_specs=[pl.BlockSpec((tm, D), lambda i: (i, 0)),
                  pl.BlockSpec((1, N_SUB * N_MOD, D), lambda i: (i // tiles_per_seq, 0, 0)),
                  _resident(g_norm.shape),
                  _resident(w_in.shape),
                  _resident(decay.shape), _resident(xi.shape), _resident(zeta.shape),
                  _resident(g_chunk.shape)] + [r[0] for r in riders],
        out_specs=out_specs + [r[1] for r in riders],
        scratch_shapes=[pltpu.VMEM((tm, widths["rq"]), BF16),
                        pltpu.VMEM((widths["rk"], tm), BF16),
                        pltpu.VMEM((tm, widths["rv"]), BF16),
                        pltpu.VMEM((tm, widths["rg"]), BF16),
                        pltpu.VMEM((RET_HEADS, widths["rq"] // RET_HEADS, widths["rv"] // RET_HEADS), F32)],
        compiler_params=_params("arbitrary"),
        name="inproj_retention",
    )(x2d, mod3, g_norm, w_in, decay, xi, zeta, g_chunk, *to_cast)


def _retention_chunk(q_ref, kt_ref, v_ref, sg_ref, o_ref, dec_ref, xi_ref, zeta_ref, gch_ref, state_ref,
                     *, heads, dk, dv):
    C = dec_ref.shape[-1]
    q = [q_ref[:, h * dk:(h + 1) * dk] for h in range(heads)]
    kt = [kt_ref[h * dk:(h + 1) * dk, :] for h in range(heads)]
    v = [v_ref[:, h * dv:(h + 1) * dv] for h in range(heads)]

    def first_matmuls(h):
        s = jnp.dot(q[h], kt[h], preferred_element_type=F32)
        cross = jnp.dot(q[h], state_ref[h].astype(BF16), preferred_element_type=F32)
        return s, cross

    def stacked_lhs(h, s):
        intra = (s * dec_ref[h]).astype(BF16)
        kz_t = (kt[h].astype(F32) * zeta_ref[h]).astype(BF16)
        return jnp.concatenate([intra, kz_t], axis=0)

    def finish(h, both, cross):
        state_ref[h] = state_ref[h] * gch_ref[h] + both[C:, :]
        xi = jnp.concatenate([xi_ref[h]] * (dv // LANES), axis=1)
        o = both[:C, :] + cross * xi
        mu = jnp.mean(o, axis=-1, keepdims=True)
        d = o - mu
        var = jnp.mean(d * d, axis=-1, keepdims=True)
        on = d * lax.rsqrt(var + EPS)
        gate = sg_ref[:, h * dv:(h + 1) * dv].astype(F32)
        o_ref[:, h * dv:(h + 1) * dv] = (on * gate).astype(BF16)

    for h0 in range(0, heads, 2):
        pair = range(h0, min(h0 + 2, heads))
        first = [first_matmuls(h) for h in pair]
        yield
        both = [jnp.dot(stacked_lhs(h, s), v[h], preferred_element_type=F32)
                for h, (s, _) in zip(pair, first)]
        yield
        for h, b, (_, cross) in zip(pair, both, first):
            finish(h, b, cross)
            yield


def _retention_tables(heads, C):
    hh = jnp.arange(heads, dtype=F32)
    log_g = jnp.log1p(-jnp.exp2(-5.0 - hh))
    pos = jnp.arange(C, dtype=F32)
    rel = pos[:, None] - pos[None, :]
    decay = jnp.where(rel[None] >= 0, jnp.exp(jnp.maximum(rel, 0.0)[None] * log_g[:, None, None]), 0.0)
    xi = jnp.exp((pos + 1.0)[None, :] * log_g[:, None])[..., None]
    zeta = jnp.exp((C - 1.0 - pos)[None, :] * log_g[:, None])[:, None, :]
    g_chunk = jnp.exp(C * log_g)[:, None, None]
    return decay, jnp.broadcast_to(xi, (heads, C, LANES)), zeta, g_chunk


ATTN_BUFFER_SETS = 2
DIFF_HEADS_PER_STEP = 2


def _diff_kernel(qt_ref, k_ref, vt_in_ref, lam_ref, subln_ref, slope_ref, *rest,
                 T, seq, dh, lam_init, n_step_riders, n_row_riders):
    n_riders = n_step_riders + n_row_riders
    rider_in, o_ref, rider_out = rest[:n_riders], rest[n_riders], rest[n_riders + 1:2 * n_riders + 1]
    scratch = rest[2 * n_riders + 1:]
    _run_riders(rider_in[:n_step_riders], rider_out[:n_step_riders])

    @pl.when(pl.program_id(1) == 0)
    def _():
        _run_riders(rider_in[n_step_riders:], rider_out[n_step_riders:])

    lp = lam_ref[...]
    lam = (jnp.exp(jnp.sum(lp[0:1] * lp[1:2], axis=-1, keepdims=True))
           - jnp.exp(jnp.sum(lp[2:3] * lp[3:4], axis=-1, keepdims=True)) + lam_init)
    gain = subln_ref[...] * (1.0 - lam_init)
    hw = 2 * dh
    for hh in range(DIFF_HEADS_PER_STEP):
        cols = slice(hh * hw, (hh + 1) * hw)
        _diff_head(qt_ref.at[0, cols, :], k_ref.at[0, :, cols], vt_in_ref.at[0, cols, :], slope_ref[hh],
                   lam, gain, o_ref.at[0, :, cols], *scratch, T=T, seq=seq, dh=dh)


def _diff_head(qt_ref, k_ref, vt_in_ref, slope, lam, gain, o_ref,
               vt_ref, wq_ref, kaug_ref, acc_ref, m_ref, *buf_refs, T, seq, dh):
    nblk = seq // T
    dv = 2 * dh
    dv_ext = dv + BF16_SUBLANES

    row = lax.broadcasted_iota(jnp.int32, (T, LANES), 0)
    lane = lax.broadcasted_iota(jnp.int32, (T, LANES), 1)
    kaug_ref[...] = jnp.where(lane < 3, row, 0).astype(F32).astype(BF16)
    s_hi = slope.astype(BF16).astype(F32)
    s_mid = (slope - s_hi).astype(BF16).astype(F32)
    s_lo = slope - s_hi - s_mid
    aug_row = lax.broadcasted_iota(jnp.int32, (LANES, 2 * T), 0)
    slope_rows = jnp.where(aug_row == 0, s_hi, jnp.where(aug_row == 1, s_mid,
                           jnp.where(aug_row == 2, s_lo, 0.0))).astype(BF16)
    ones_rows = jnp.where(lax.broadcasted_iota(jnp.int32, (BF16_SUBLANES, T), 0) == 0, 1.0, 0.0).astype(BF16)
    feat_id = lax.broadcasted_iota(jnp.int32, (dv, T), 0)
    for j in range(nblk):
        cols = slice(j * T, (j + 1) * T)
        vt_ref[j, 0:dv, :] = vt_in_ref[:, cols]
        vt_ref[j, dv:dv_ext, :] = ones_rows
        qt = qt_ref[:, cols]
        zero = jnp.zeros_like(qt)
        wq_ref[j, 0:dv, :] = jnp.concatenate([jnp.where(feat_id < dh, qt, zero),
                                              jnp.where(feat_id >= dh, qt, zero)], axis=1)
        wq_ref[j, dv:dv + LANES, :] = slope_rows

    def scores(qi, j):
        k_blk = k_ref[j * T:(j + 1) * T, :]
        return jnp.dot(jnp.concatenate([k_blk, kaug_ref[...]], axis=1), wq_ref[qi],
                       preferred_element_type=F32)

    key_id = lax.broadcasted_iota(jnp.int32, (T, 2 * T), 0)
    col_id = lax.broadcasted_iota(jnp.int32, (T, 2 * T), 1)
    causal = key_id <= jnp.where(col_id >= T, col_id - T, col_id)

    blocks = [(q, q) for q in range(nblk)] + [(q, j) for q in range(1, nblk) for j in range(q)]
    bufs = [buf_refs[4 * u:4 * u + 4] for u in range(len(buf_refs) // 4)]

    def produce_scores(s):
        qi, j = blocks[s]
        t_ref, tmax_ref, _, _ = bufs[s % len(bufs)]
        t = scores(qi, j)
        if j == qi:
            t = jnp.where(causal, t, NEG)
        t_ref[...] = t
        tmax_ref[...] = jnp.max(t, axis=0, keepdims=True)

    def retire(s):
        qi, j = blocks[s]
        _, _, p_ref, alpha_ref = bufs[s % len(bufs)]
        pv = jnp.dot(vt_ref[j], p_ref[...], preferred_element_type=F32)
        acc_ref[qi] = pv if j == qi else alpha_ref[...] * acc_ref[qi] + pv

    def softmax(s):
        qi, j = blocks[s]
        t_ref, tmax_ref, p_ref, alpha_ref = bufs[s % len(bufs)]
        t = t_ref[...]
        if j == qi:
            m_new = tmax_ref[...]
            p_ref[...] = jnp.exp2(t - m_new).astype(BF16)
        else:
            m_old = m_ref[qi]
            shift = slope * float((j - qi) * T)
            m_new = jnp.maximum(m_old, tmax_ref[...] + shift)
            alpha_ref[...] = jnp.exp2(m_old - m_new)
            p_ref[...] = jnp.exp2(t + (shift - m_new)).astype(BF16)
        m_ref[qi] = m_new

    produce_scores(0)
    for s in range(len(blocks)):
        if s + 1 < len(blocks):
            produce_scores(s + 1)
        if s > 0:
            retire(s - 1)
        softmax(s)
    retire(len(blocks) - 1)

    gain = jnp.broadcast_to(gain, (dv, T))
    for qi in range(nblk):
        acc = acc_ref[qi]
        r = 1.0 / acc[dv:dv + 1, :]
        out = acc[0:dv, :T] * r[:, :T] - acc[0:dv, T:] * (lam * r[:, T:])
        ms = jnp.mean(out * out, axis=0, keepdims=True)
        o_ref[qi * T:(qi + 1) * T, :] = (out * lax.rsqrt(ms + EPS) * gain).T.astype(BF16)


def _diff_attention(dqt, dk, dvt, diff_lambda, subln, *, heads, lam_init, cast_by_step=(), cast_by_row=(),
                    T=256):
    B, S, W = dk.shape
    hw = W // heads
    dh = hw // 2
    nblk = S // T
    assert T <= 256 and hw == LANES
    slopes = jnp.exp2(-8.0 * (jnp.arange(heads, dtype=F32) + 1.0) / heads) * LOG2E
    hps = DIFF_HEADS_PER_STEP
    groups = heads // hps
    assert groups * hps == heads
    head_spec = pl.BlockSpec((1, S, hps * hw), lambda b, g: (b, 0, g))
    head_t_spec = pl.BlockSpec((1, hps * hw, S), lambda b, g: (b, g, 0))
    riders = ([_rider(w, B * groups, lambda b, g: b * groups + g) for w in cast_by_step]
              + [_rider(w, B, lambda b, g: b) for w in cast_by_row])
    return pl.pallas_call(
        functools.partial(_diff_kernel, T=T, seq=S, dh=dh, lam_init=lam_init,
                          n_step_riders=len(cast_by_step), n_row_riders=len(cast_by_row)),
        out_shape=[jax.ShapeDtypeStruct((B, S, W), BF16)] + [r[2] for r in riders],
        grid=(B, groups),
        in_specs=[head_t_spec, head_spec, head_t_spec,
                  _resident(diff_lambda.shape),
                  _resident((hw, 1)),
                  pl.BlockSpec((hps, 1, 1), lambda b, g: (g, 0, 0))] + [r[0] for r in riders],
        out_specs=[head_spec] + [r[1] for r in riders],
        scratch_shapes=[pltpu.VMEM((nblk, hw + BF16_SUBLANES, T), BF16),
                        pltpu.VMEM((nblk, hw + LANES, 2 * T), BF16),
                        pltpu.VMEM((T, LANES), BF16),
                        pltpu.VMEM((nblk, hw + BF16_SUBLANES, 2 * T), F32),
                        pltpu.VMEM((nblk, 1, 2 * T), F32)]
                       + [pltpu.VMEM((T, 2 * T), F32),
                          pltpu.VMEM((1, 2 * T), F32),
                          pltpu.VMEM((T, 2 * T), BF16),
                          pltpu.VMEM((1, 2 * T), F32)] * ATTN_BUFFER_SETS,
        compiler_params=_params("arbitrary", "arbitrary"),
        name="diffattn",
    )(dqt, dk, dvt, diff_lambda, subln.reshape(hw, 1), slopes.reshape(heads, 1, 1),
      *cast_by_step, *cast_by_row)


def kernel(x, c, w_cond, b_cond, g_norm, w_ffn1_in, w_ffn1_out, w_in, w_ret_out, diff_lambda,
           diff_subln, w_diff_out, w_out, w_ffn2_in, w_ffn2_out, g_final):
    B, S, D = x.shape
    depth = w_cond.shape[0]
    diff_heads = D // (2 * DIFF_HEAD_DIM)
    xf = x.reshape(B * S, D)
    for l in range(depth):
        lam_init = 0.8 - 0.6 * math.exp(-0.3 * l)
        mod3 = _mod(c, w_cond[l], b_cond[l]).reshape(B, N_SUB * N_MOD, D)
        last = l == depth - 1

        xf, w_in_bf = _ffn(xf, mod3, g_norm[l], w_ffn1_in[l].astype(BF16), w_ffn1_out[l].astype(BF16),
                           None, sub=0, seq=S, to_cast=(w_in[l],))

        ret, dqt, dk, dvt, gs, w_ro_bf, w_do_bf, w_o_bf = _inproj(
            xf, mod3, g_norm[l], w_in_bf, batch=B, seq=S,
            to_cast=(w_ret_out[l], w_diff_out[l], w_out[l]))
        dif, w_f2i_bf, w_f2o_bf = _diff_attention(
            dqt, dk.reshape(B, S, D), dvt, diff_lambda[l], diff_subln[l], heads=diff_heads,
            lam_init=lam_init, cast_by_step=(w_ffn2_in[l],), cast_by_row=(w_ffn2_out[l],))
        (xf,) = _ffn(xf, mod3, g_norm[l], w_f2i_bf, w_f2o_bf, g_final if last else None, sub=2, seq=S,
                     mix=(ret, dif.reshape(B * S, -1), gs, w_ro_bf, w_do_bf, w_o_bf))
    if depth == 0:
        raise ValueError("depth must be positive")
    return xf.reshape(B, S, D)
```

```python
import functools
import math

import jax
import jax.numpy as jnp
from jax import lax
from jax.experimental import pallas as pl
from jax.experimental.pallas import tpu as pltpu

F32 = jnp.float32
BF16 = jnp.bfloat16
EPS = 1e-6
NEG = -1e30
LOG2E = math.log2(math.e)

RET_HEADS = 4
RET_CHUNK = 256
INPROJ_TILES_PER_STEP = 2
DIFF_HEAD_DIM = 64
N_SUB = 3
N_MOD = 3

VMEM_LIMIT_BYTES = 56 * 1024 * 1024
BF16_SUBLANES = 16
LANES = 128


def _resident(shape):
    nd = len(shape)
    return pl.BlockSpec(shape, lambda *_: (0,) * nd, pipeline_mode=pl.Buffered(1))


def _params(*sem):
    return pltpu.CompilerParams(dimension_semantics=sem, vmem_limit_bytes=VMEM_LIMIT_BYTES)


def _silu(a):
    return a / (1.0 + jnp.exp(-a))


def _sigmoid(a):
    return 1.0 / (1.0 + jnp.exp(-a))


def _modulated(x, mod_ref, g_ref, sub):
    shift = mod_ref[0, N_MOD * sub:N_MOD * sub + 1, :]
    scale = mod_ref[0, N_MOD * sub + 1:N_MOD * sub + 2, :]
    ms = jnp.mean(x * x, axis=-1, keepdims=True)
    gain = g_ref[sub:sub + 1, :] * (1.0 + scale)
    return (x * lax.rsqrt(ms + EPS) * gain + shift).astype(BF16)


def _rider(w, nblocks, block_of_step):
    rows = w.shape[0] // nblocks
    assert rows * nblocks == w.shape[0] and rows % BF16_SUBLANES == 0
    spec = pl.BlockSpec((rows, w.shape[1]), lambda *g: (block_of_step(*g), 0))
    return spec, spec, jax.ShapeDtypeStruct(w.shape, BF16)


def _run_riders(in_refs, out_refs):
    for src, dst in zip(in_refs, out_refs):
        dst[...] = src[...].astype(BF16)


def _mod_kernel(c_ref, w_ref, b_ref, o_ref):
    c_act = _silu(c_ref[...]).astype(BF16)
    o_ref[...] = jnp.dot(c_act, w_ref[...].astype(BF16), preferred_element_type=F32) + b_ref[...]


def _mod(c, w, b):
    B, D = c.shape
    N = w.shape[1]
    tn = 2304
    assert N % tn == 0
    return pl.pallas_call(
        _mod_kernel,
        out_shape=jax.ShapeDtypeStruct((B, N), F32),
        grid=(N // tn,),
        in_specs=[pl.BlockSpec((B, D), lambda j: (0, 0)),
                  pl.BlockSpec((D, tn), lambda j: (0, j)),
                  pl.BlockSpec((1, tn), lambda j: (0, j))],
        out_specs=pl.BlockSpec((B, tn), lambda j: (0, j)),
        compiler_params=_params("arbitrary"),
        name="mod",
    )(c, w, b.reshape(1, N))


def _mixer_merge(x, mod_ref, ret_ref, dif_ref, gs_ref, wro_ref, wdo_ref, wo_ref):
    D = x.shape[-1]
    y_ret = jnp.dot(ret_ref[...], wro_ref[...], preferred_element_type=F32)
    y_dif = jnp.dot(dif_ref[...], wdo_ref[...], preferred_element_type=F32)
    y = gs_ref[:, :D].astype(F32) * y_ret + gs_ref[:, D:].astype(F32) * y_dif
    gate = mod_ref[0, N_MOD * 1 + 2:N_MOD * 1 + 3, :]
    return x + gate * jnp.dot(y.astype(BF16), wo_ref[...], preferred_element_type=F32)


def _ffn_kernel(x_ref, mod_ref, g_ref, win_ref, wout_ref, *rest, sub, d_ff, tf, final, mix, n_riders):
    rest = list(rest)
    gfin_ref = rest.pop(0) if final else None
    mix_refs = [rest.pop(0) for _ in range(6)] if mix else None
    rider_in = [rest.pop(0) for _ in range(n_riders)]
    o_ref = rest.pop(0)
    rider_out = [rest.pop(0) for _ in range(n_riders)]
    (act_ref,) = rest
    _run_riders(rider_in, rider_out)
    tm = act_ref.shape[1]
    for tile in range(act_ref.shape[0]):
        rows = slice(tile * tm, (tile + 1) * tm)
        x = x_ref[rows, :]
        if mix:
            x = _mixer_merge(x, mod_ref, *[r.at[rows, :] for r in mix_refs[:3]], *mix_refs[3:])
        h = _modulated(x, mod_ref, g_ref, sub)
        for c in range(d_ff // tf):
            a = jnp.dot(h, win_ref[:, c * tf:(c + 1) * tf], preferred_element_type=F32)
            b = jnp.dot(h, win_ref[:, d_ff + c * tf:d_ff + (c + 1) * tf], preferred_element_type=F32)
            act_ref[tile, :, c * tf:(c + 1) * tf] = (_silu(a) * b).astype(BF16)
        y = jnp.dot(act_ref[tile], wout_ref[...], preferred_element_type=F32)
        gate = mod_ref[0, N_MOD * sub + 2:N_MOD * sub + 3, :]
        out = x + (0.5 * gate) * y
        if final:
            ms = jnp.mean(out * out, axis=-1, keepdims=True)
            out = out * lax.rsqrt(ms + EPS) * gfin_ref[...]
        o_ref[rows, :] = out


def _ffn(x2d, mod3, g_norm, w_in, w_out, g_final, *, sub, seq, mix=None, to_cast=(), tiles_per_step=1,
         tile=512, tf=256):
    M, D = x2d.shape
    d_ff = w_out.shape[0]
    tm = tiles_per_step * tile
    tiles_per_seq = seq // tm
    assert tiles_per_seq * tm == seq
    final = g_final is not None
    row = lambda width: pl.BlockSpec((tm, width), lambda i: (i, 0))
    in_specs = [row(D),
                pl.BlockSpec((1, N_SUB * N_MOD, D), lambda i: (i // tiles_per_seq, 0, 0)),
                _resident(g_norm.shape),
                _resident(w_in.shape),
                _resident(w_out.shape)]
    args = [x2d, mod3, g_norm, w_in, w_out]
    if final:
        in_specs.append(_resident((1, D)))
        args.append(g_final.reshape(1, D))
    if mix is not None:
        in_specs += [row(a.shape[1]) for a in mix[:3]] + [_resident(w.shape) for w in mix[3:]]
        args += list(mix)
    riders = [_rider(w, M // tm, lambda i: i) for w in to_cast]
    return pl.pallas_call(
        functools.partial(_ffn_kernel, sub=sub, d_ff=d_ff, tf=tf, final=final, mix=mix is not None,
                          n_riders=len(riders)),
        out_shape=[jax.ShapeDtypeStruct((M, D), F32)] + [r[2] for r in riders],
        grid=(M // tm,),
        in_specs=in_specs + [r[0] for r in riders],
        out_specs=[row(D)] + [r[1] for r in riders],
        scratch_shapes=[pltpu.VMEM((tiles_per_step, tile, d_ff), BF16)],
        compiler_params=_params("arbitrary"),
        name="ffn%d" % sub,
    )(*args, *to_cast)


def _inproj_segments(D):
    return (("rq", D, None, "ret"), ("rk", D, "ret_k_scale", "ret_t"), ("rv", 2 * D, None, "ret"),
            ("rg", 2 * D, "silu", "ret"), ("dq", D, "diff_q_scale", "out_t"), ("dk", D, None, "out"),
            ("dv", D, None, "out_t"), ("gates", 2 * D, "sigmoid", "out"))


def _inproj_kernel(x_ref, mod_ref, g_ref, w_ref, dec_ref, xi_ref, zeta_ref, gch_ref, *rest,
                   segs, tn, ret_k_scale, diff_q_scale, steps_per_seq, ret_heads, n_riders):
    rest = list(rest)
    rider_in = [rest.pop(0) for _ in range(n_riders)]
    ret_ref = rest.pop(0)
    out_refs = {s[0]: rest.pop(0) for s in segs if s[3].startswith("out")}
    rider_out = [rest.pop(0) for _ in range(n_riders)]
    state_ref = rest.pop()
    ret_names = [s[0] for s in segs if s[3].startswith("ret")]
    scratch_sets = [dict(zip(ret_names, rest[i:i + len(ret_names)]))
                    for i in range(0, len(rest), len(ret_names))]
    _run_riders(rider_in, rider_out)

    @pl.when(lax.rem(pl.program_id(0), steps_per_seq) == 0)
    def _():
        state_ref[...] = jnp.zeros_like(state_ref)

    first_col, col = {}, 0
    for name, width, _, _ in segs:
        first_col[name] = col
        col += width
    ret_segs = [s for s in segs if s[3].startswith("ret")]
    out_segs = sorted((s for s in segs if not s[3].startswith("ret")),
                      key=lambda s: (s[2] is None, s[3] == "out"))
    C = dec_ref.shape[-1]

    for tile, scratch in enumerate(scratch_sets):
        rows = slice(tile * C, (tile + 1) * C)
        dest = dict(scratch)
        for name, _, _, kind in out_segs:
            dest[name] = out_refs[name].at[0, :, rows] if kind == "out_t" else out_refs[name].at[rows, :]
        h = _modulated(x_ref[rows, :], mod_ref, g_ref, 1)
        retention_stages = iter(())
        for name, width, epi, kind in ret_segs + out_segs:
            o_ref = dest[name]
            col = first_col[name]
            for c in range(width // tn):
                next(retention_stages, None)
                y = jnp.dot(h, w_ref[:, col + c * tn:col + (c + 1) * tn], preferred_element_type=F32)
                if epi == "silu":
                    y = _silu(y)
                elif epi == "sigmoid":
                    y = _sigmoid(y)
                elif epi == "ret_k_scale":
                    y = y * ret_k_scale
                elif epi == "diff_q_scale":
                    y = y * diff_q_scale
                if kind.endswith("_t"):
                    o_ref[c * tn:(c + 1) * tn, :] = y.astype(BF16).T
                else:
                    o_ref[:, c * tn:(c + 1) * tn] = y.astype(BF16)
            if name == ret_segs[-1][0]:
                qk_w, v_w = dest["rq"].shape[1], dest["rv"].shape[1]
                retention_stages = _retention_chunk(
                    dest["rq"], dest["rk"], dest["rv"], dest["rg"], ret_ref.at[rows, :],
                    dec_ref, xi_ref, zeta_ref, gch_ref, state_ref,
                    heads=ret_heads, dk=qk_w // ret_heads, dv=v_w // ret_heads)
        for _ in retention_stages:
            pass


def _inproj(x2d, mod3, g_norm, w_in, *, batch, seq, to_cast=(), tn=512):
    M, D = x2d.shape
    C = RET_CHUNK
    tm = INPROJ_TILES_PER_STEP * C
    segs = _inproj_segments(D)
    assert sum(s[1] for s in segs) == w_in.shape[1]
    tiles_per_seq = seq // tm
    assert tiles_per_seq * tm == seq
    widths = {s[0]: s[1] for s in segs}
    decay, xi, zeta, g_chunk = _retention_tables(RET_HEADS, C)
    kern = functools.partial(
        _inproj_kernel, segs=segs, tn=tn, steps_per_seq=tiles_per_seq, ret_heads=RET_HEADS,
        ret_k_scale=float((D // RET_HEADS) ** -0.5),
        diff_q_scale=LOG2E * DIFF_HEAD_DIM ** -0.5,
        n_riders=len(to_cast))
    row_spec = lambda width: pl.BlockSpec((tm, width), lambda i: (i, 0))
    out_shape, out_specs = [jax.ShapeDtypeStruct((M, widths["rv"]), BF16)], [row_spec(widths["rv"])]
    for _, width, _, kind in segs:
        if kind == "out_t":
            out_shape.append(jax.ShapeDtypeStruct((batch, width, seq), BF16))
            out_specs.append(pl.BlockSpec((1, width, tm),
                                          lambda i: (i // tiles_per_seq, 0, i % tiles_per_seq)))
        elif kind == "out":
            out_shape.append(jax.ShapeDtypeStruct((M, width), BF16))
            out_specs.append(row_spec(width))
    riders = [_rider(w, M // tm, lambda i: i) for w in to_cast]
    return pl.pallas_call(
        kern,
        out_shape=out_shape + [r[2] for r in riders],
        grid=(M // tm,),
        in_specs=[pl.BlockSpec((tm, D), lambda i: (i, 0)),
                  pl.BlockSpec((1, N_SUB * N_MOD, D), lambda i: (i // tiles_per_seq, 0, 0)),
                  _resident(g_norm.shape),
                  _resident(w_in.shape),
                  _resident(decay.shape), _resident(xi.shape), _resident(zeta.shape),
                  _resident(g_chunk.shape)] + [r[0] for r in riders],
        out_specs=out_specs + [r[1] for r in riders],
        scratch_shapes=[pltpu.VMEM((C, widths["rq"]), BF16),
                        pltpu.VMEM((widths["rk"], C), BF16),
                        pltpu.VMEM((C, widths["rv"]), BF16),
                        pltpu.VMEM((C, widths["rg"]), BF16)
                        ] * INPROJ_TILES_PER_STEP
                       + [pltpu.VMEM((RET_HEADS, widths["rq"] // RET_HEADS, widths["rv"] // RET_HEADS), F32)],
        compiler_params=_params("arbitrary"),
        name="inproj_retention",
    )(x2d, mod3, g_norm, w_in, decay, xi, zeta, g_chunk, *to_cast)


def _retention_chunk(q_ref, kt_ref, v_ref, sg_ref, o_ref, dec_ref, xi_ref, zeta_ref, gch_ref, state_ref,
                     *, heads, dk, dv):
    C = dec_ref.shape[-1]
    q = [q_ref[:, h * dk:(h + 1) * dk] for h in range(heads)]
    kt = [kt_ref[h * dk:(h + 1) * dk, :] for h in range(heads)]
    v = [v_ref[:, h * dv:(h + 1) * dv] for h in range(heads)]

    def first_matmuls(h):
        s = jnp.dot(q[h], kt[h], preferred_element_type=F32)
        cross = jnp.dot(q[h], state_ref[h].astype(BF16), preferred_element_type=F32)
        return s, cross

    def stacked_lhs(h, s):
        intra = (s * dec_ref[h]).astype(BF16)
        kz_t = (kt[h].astype(F32) * zeta_ref[h]).astype(BF16)
        return jnp.concatenate([intra, kz_t], axis=0)

    def finish(h, both, cross):
        state_ref[h] = state_ref[h] * gch_ref[h] + both[C:, :]
        xi = jnp.concatenate([xi_ref[h]] * (dv // LANES), axis=1)
        o = both[:C, :] + cross * xi
        mu = jnp.mean(o, axis=-1, keepdims=True)
        d = o - mu
        var = jnp.mean(d * d, axis=-1, keepdims=True)
        on = d * lax.rsqrt(var + EPS)
        gate = sg_ref[:, h * dv:(h + 1) * dv].astype(F32)
        o_ref[:, h * dv:(h + 1) * dv] = (on * gate).astype(BF16)

    for h0 in range(0, heads, 2):
        pair = range(h0, min(h0 + 2, heads))
        first = [first_matmuls(h) for h in pair]
        yield
        both = [jnp.dot(stacked_lhs(h, s), v[h], preferred_element_type=F32)
                for h, (s, _) in zip(pair, first)]
        yield
        for h, b, (_, cross) in zip(pair, both, first):
            finish(h, b, cross)
            yield


def _retention_tables(heads, C):
    hh = jnp.arange(heads, dtype=F32)
    log_g = jnp.log1p(-jnp.exp2(-5.0 - hh))
    pos = jnp.arange(C, dtype=F32)
    rel = pos[:, None] - pos[None, :]
    decay = jnp.where(rel[None] >= 0, jnp.exp(jnp.maximum(rel, 0.0)[None] * log_g[:, None, None]), 0.0)
    xi = jnp.exp((pos + 1.0)[None, :] * log_g[:, None])[..., None]
    zeta = jnp.exp((C - 1.0 - pos)[None, :] * log_g[:, None])[:, None, :]
    g_chunk = jnp.exp(C * log_g)[:, None, None]
    return decay, jnp.broadcast_to(xi, (heads, C, LANES)), zeta, g_chunk


ATTN_BUFFER_SETS = 2
DIFF_HEADS_PER_STEP = 2


def _diff_kernel(qt_ref, k_ref, vt_in_ref, lam_ref, subln_ref, slope_ref, *rest,
                 T, seq, dh, lam_init, n_step_riders, n_row_riders):
    n_riders = n_step_riders + n_row_riders
    rider_in, o_ref, rider_out = rest[:n_riders], rest[n_riders], rest[n_riders + 1:2 * n_riders + 1]
    scratch = rest[2 * n_riders + 1:]
    _run_riders(rider_in[:n_step_riders], rider_out[:n_step_riders])

    @pl.when(pl.program_id(1) == 0)
    def _():
        _run_riders(rider_in[n_step_riders:], rider_out[n_step_riders:])

    lp = lam_ref[...]
    lam = (jnp.exp(jnp.sum(lp[0:1] * lp[1:2], axis=-1, keepdims=True))
           - jnp.exp(jnp.sum(lp[2:3] * lp[3:4], axis=-1, keepdims=True)) + lam_init)
    gain = subln_ref[...] * (1.0 - lam_init)
    hw = 2 * dh
    for hh in range(DIFF_HEADS_PER_STEP):
        cols = slice(hh * hw, (hh + 1) * hw)
        _diff_head(qt_ref.at[0, cols, :], k_ref.at[0, :, cols], vt_in_ref.at[0, cols, :], slope_ref[hh],
                   lam, gain, o_ref.at[0, :, cols], *scratch, T=T, seq=seq, dh=dh)


def _diff_head(qt_ref, k_ref, vt_in_ref, slope, lam, gain, o_ref,
               vt_ref, wq_ref, kaug_ref, acc_ref, m_ref, *buf_refs, T, seq, dh):
    nblk = seq // T
    dv = 2 * dh
    dv_ext = dv + BF16_SUBLANES

    row = lax.broadcasted_iota(jnp.int32, (T, LANES), 0)
    lane = lax.broadcasted_iota(jnp.int32, (T, LANES), 1)
    kaug_ref[...] = jnp.where(lane < 3, row, 0).astype(F32).astype(BF16)
    s_hi = slope.astype(BF16).astype(F32)
    s_mid = (slope - s_hi).astype(BF16).astype(F32)
    s_lo = slope - s_hi - s_mid
    aug_row = lax.broadcasted_iota(jnp.int32, (LANES, 2 * T), 0)
    slope_rows = jnp.where(aug_row == 0, s_hi, jnp.where(aug_row == 1, s_mid,
                           jnp.where(aug_row == 2, s_lo, 0.0))).astype(BF16)
    ones_rows = jnp.where(lax.broadcasted_iota(jnp.int32, (BF16_SUBLANES, T), 0) == 0, 1.0, 0.0).astype(BF16)
    feat_id = lax.broadcasted_iota(jnp.int32, (dv, T), 0)
    for j in range(nblk):
        cols = slice(j * T, (j + 1) * T)
        vt_ref[j, 0:dv, :] = vt_in_ref[:, cols]
        vt_ref[j, dv:dv_ext, :] = ones_rows
        qt = qt_ref[:, cols]
        zero = jnp.zeros_like(qt)
        wq_ref[j, 0:dv, :] = jnp.concatenate([jnp.where(feat_id < dh, qt, zero),
                                              jnp.where(feat_id >= dh, qt, zero)], axis=1)
        wq_ref[j, dv:dv + LANES, :] = slope_rows

    def scores(qi, j):
        k_blk = k_ref[j * T:(j + 1) * T, :]
        return jnp.dot(jnp.concatenate([k_blk, kaug_ref[...]], axis=1), wq_ref[qi],
                       preferred_element_type=F32)

    key_id = lax.broadcasted_iota(jnp.int32, (T, 2 * T), 0)
    col_id = lax.broadcasted_iota(jnp.int32, (T, 2 * T), 1)
    causal = key_id <= jnp.where(col_id >= T, col_id - T, col_id)

    blocks = [(q, q) for q in range(nblk)] + [(q, j) for q in range(1, nblk) for j in range(q)]
    bufs = [buf_refs[4 * u:4 * u + 4] for u in range(len(buf_refs) // 4)]

    def produce_scores(s):
        qi, j = blocks[s]
        t_ref, tmax_ref, _, _ = bufs[s % len(bufs)]
        t = scores(qi, j)
        if j == qi:
            t = jnp.where(causal, t, NEG)
        t_ref[...] = t
        tmax_ref[...] = jnp.max(t, axis=0, keepdims=True)

    def retire(s):
        qi, j = blocks[s]
        _, _, p_ref, alpha_ref = bufs[s % len(bufs)]
        pv = jnp.dot(vt_ref[j], p_ref[...], preferred_element_type=F32)
        acc_ref[qi] = pv if j == qi else alpha_ref[...] * acc_ref[qi] + pv

    def softmax(s):
        qi, j = blocks[s]
        t_ref, tmax_ref, p_ref, alpha_ref = bufs[s % len(bufs)]
        t = t_ref[...]
        if j == qi:
            m_new = tmax_ref[...]
            p_ref[...] = jnp.exp2(t - m_new).astype(BF16)
        else:
            m_old = m_ref[qi]
            shift = slope * float((j - qi) * T)
            m_new = jnp.maximum(m_old, tmax_ref[...] + shift)
            alpha_ref[...] = jnp.exp2(m_old - m_new)
            p_ref[...] = jnp.exp2(t + (shift - m_new)).astype(BF16)
        m_ref[qi] = m_new

    produce_scores(0)
    for s in range(len(blocks)):
        if s + 1 < len(blocks):
            produce_scores(s + 1)
        if s > 0:
            retire(s - 1)
        softmax(s)
    retire(len(blocks) - 1)

    gain = jnp.broadcast_to(gain, (dv, T))
    for qi in range(nblk):
        acc = acc_ref[qi]
        r = 1.0 / acc[dv:dv + 1, :]
        out = acc[0:dv, :T] * r[:, :T] - acc[0:dv, T:] * (lam * r[:, T:])
        ms = jnp.mean(out * out, axis=0, keepdims=True)
        o_ref[qi * T:(qi + 1) * T, :] = (out * lax.rsqrt(ms + EPS) * gain).T.astype(BF16)


def _diff_attention(dqt, dk, dvt, diff_lambda, subln, *, heads, lam_init, cast_by_step=(), cast_by_row=(),
                    T=256):
    B, S, W = dk.shape
    hw = W // heads
    dh = hw // 2
    nblk = S // T
    assert T <= 256 and hw == LANES
    slopes = jnp.exp2(-8.0 * (jnp.arange(heads, dtype=F32) + 1.0) / heads) * LOG2E
    hps = DIFF_HEADS_PER_STEP
    groups = heads // hps
    assert groups * hps == heads
    head_spec = pl.BlockSpec((1, S, hps * hw), lambda b, g: (b, 0, g))
    head_t_spec = pl.BlockSpec((1, hps * hw, S), lambda b, g: (b, g, 0))
    riders = ([_rider(w, B * groups, lambda b, g: b * groups + g) for w in cast_by_step]
              + [_rider(w, B, lambda b, g: b) for w in cast_by_row])
    return pl.pallas_call(
        functools.partial(_diff_kernel, T=T, seq=S, dh=dh, lam_init=lam_init,
                          n_step_riders=len(cast_by_step), n_row_riders=len(cast_by_row)),
        out_shape=[jax.ShapeDtypeStruct((B, S, W), BF16)] + [r[2] for r in riders],
        grid=(B, groups),
        in_specs=[head_t_spec, head_spec, head_t_spec,
                  _resident(diff_lambda.shape),
                  _resident((hw, 1)),
                  pl.BlockSpec((hps, 1, 1), lambda b, g: (g, 0, 0))] + [r[0] for r in riders],
        out_specs=[head_spec] + [r[1] for r in riders],
        scratch_shapes=[pltpu.VMEM((nblk, hw + BF16_SUBLANES, T), BF16),
                        pltpu.VMEM((nblk, hw + LANES, 2 * T), BF16),
                        pltpu.VMEM((T, LANES), BF16),
                        pltpu.VMEM((nblk, hw + BF16_SUBLANES, 2 * T), F32),
                        pltpu.VMEM((nblk, 1, 2 * T), F32)]
                       + [pltpu.VMEM((T, 2 * T), F32),
                          pltpu.VMEM((1, 2 * T), F32),
                          pltpu.VMEM((T, 2 * T), BF16),
                          pltpu.VMEM((1, 2 * T), F32)] * ATTN_BUFFER_SETS,
        compiler_params=_params("arbitrary", "arbitrary"),
        name="diffattn",
    )(dqt, dk, dvt, diff_lambda, subln.reshape(hw, 1), slopes.reshape(heads, 1, 1),
      *cast_by_step, *cast_by_row)


def kernel(x, c, w_cond, b_cond, g_norm, w_ffn1_in, w_ffn1_out, w_in, w_ret_out, diff_lambda,
           diff_subln, w_diff_out, w_out, w_ffn2_in, w_ffn2_out, g_final):
    B, S, D = x.shape
    depth = w_cond.shape[0]
    diff_heads = D // (2 * DIFF_HEAD_DIM)
    xf = x.reshape(B * S, D)
    for l in range(depth):
        lam_init = 0.8 - 0.6 * math.exp(-0.3 * l)
        mod3 = _mod(c, w_cond[l], b_cond[l]).reshape(B, N_SUB * N_MOD, D)
        last = l == depth - 1

        xf, w_in_bf = _ffn(xf, mod3, g_norm[l], w_ffn1_in[l].astype(BF16), w_ffn1_out[l].astype(BF16),
                           None, sub=0, seq=S, to_cast=(w_in[l],), tiles_per_step=2)

        ret, dqt, dk, dvt, gs, w_ro_bf, w_do_bf, w_o_bf = _inproj(
            xf, mod3, g_norm[l], w_in_bf, batch=B, seq=S,
            to_cast=(w_ret_out[l], w_diff_out[l], w_out[l]))
        dif, w_f2i_bf, w_f2o_bf = _diff_attention(
            dqt, dk.reshape(B, S, D), dvt, diff_lambda[l], diff_subln[l], heads=diff_heads,
            lam_init=lam_init, cast_by_step=(w_ffn2_in[l],), cast_by_row=(w_ffn2_out[l],))
        (xf,) = _ffn(xf, mod3, g_norm[l], w_f2i_bf, w_f2o_bf, g_final if last else None, sub=2, seq=S,
                     mix=(ret, dif.reshape(B * S, -1), gs, w_ro_bf, w_do_bf, w_o_bf))
    if depth == 0:
        raise ValueError("depth must be positive")
    return xf.reshape(B, S, D)
```
